```python
import jax, jax.numpy as jnp
from jax import lax
import numpy as np

D_MODEL = 4096
BATCH = 4
SEQ = 2048
DEPTH = 4
DEC_BATCH = 128
DEC_SEQ = 4
PAST_LEN = 16384
PAGE_SIZE = 128

D_A = D_MODEL // 2
H_A = 8
CHUNK = 128
D_B = D_MODEL - D_A
HEAD_B = 64
H_B = D_B // HEAD_B
LORA_W = max(32, round(D_B ** 0.5 * 1.8 / 32) * 32)
LORA_A = max(32, round(D_B ** 0.5 * 1.8 / 32) * 32)
LORA_V = max(32, round(D_B ** 0.5 * 1.3 / 32) * 32)
LORA_G = max(32, round(D_B ** 0.6 * 0.6 / 32) * 32)
D_FF = 4 * D_MODEL
P_A = 2 * D_A
P_B = 4 * D_B
P_IN = P_A + P_B
N_MOD = 6
EPS_RMS = 1e-6
EPS_LN = 1e-5
EPS_GN = 64e-5

kernel_name = "hybrid_sgu_rwkv7_decoder_step"


def rms_norm(x, g):
    xf = x.astype(jnp.float32)
    y = xf * lax.rsqrt(jnp.mean(xf * xf, axis=-1, keepdims=True) + EPS_RMS)
    return (y * g.astype(jnp.float32)).astype(x.dtype)


def layer_norm(x, w, b):
    xf = x.astype(jnp.float32)
    mu = jnp.mean(xf, axis=-1, keepdims=True)
    var = jnp.mean(jnp.square(xf - mu), axis=-1, keepdims=True)
    y = (xf - mu) * lax.rsqrt(var + EPS_LN)
    return (y * w.astype(jnp.float32) + b.astype(jnp.float32)).astype(x.dtype)


def token_shift(z, prev):
    return jnp.concatenate([prev[:, None, :].astype(z.dtype), z[:, :-1, :]], axis=1)


def lerp(cur, prev, mu):
    return cur + (prev - cur) * mu


def chunk_spatial_mix(v, w_s, b_s, chunk_len):
    B, T, _ = v.shape
    vc = v.reshape(B, T // chunk_len, chunk_len, H_A, D_A // H_A)
    mask = jnp.tril(jnp.ones((chunk_len, chunk_len), dtype=bool))
    ws = jnp.where(mask[None], w_s[:, :chunk_len, :chunk_len], 0.0)
    s = jnp.einsum('hij,bcjhd->bcihd', ws, vc)
    s = s + b_s[:, :chunk_len].T[None, None, :, :, None]
    return s.reshape(B, T, D_A)


def wkv7_scan(r, w, k, v, a, b, S0):
    def step(S, inp):
        r_t, w_t, k_t, v_t, a_t, b_t = inp
        sa = jnp.einsum('bhij,bhj->bhi', S, a_t)
        S = (S * w_t[:, :, None, :] + sa[..., None] * b_t[:, :, None, :]
             + v_t[..., None] * k_t[:, :, None, :])
        y = jnp.einsum('bhij,bhj->bhi', S, r_t)
        return S, y
    xs = tuple(jnp.moveaxis(t.astype(jnp.float32), 1, 0) for t in (r, w, k, v, a, b))
    S, ys = lax.scan(step, S0.astype(jnp.float32), xs)
    return jnp.moveaxis(ys, 0, 1), S


def rwkv7_mix(zb, shift_prev, S0, v_first, l, mu_rkv, mu_lora, mu_vm, w0, w1, w2,
              a0, a1, a2, v0, v1, v2, g1, g2, k_k, k_a, r_k, ln_x_w, ln_x_b):
    B, T, _ = zb.shape
    zs = token_shift(zb, shift_prev)
    r_c, k_c, v_c, x_c = jnp.split(zb, 4, axis=-1)
    r_p, k_p, v_p, x_p = jnp.split(zs, 4, axis=-1)
    r = lerp(r_c, r_p, mu_rkv[l, 0])
    k = lerp(k_c, k_p, mu_rkv[l, 1])
    v = lerp(v_c, v_p, mu_rkv[l, 2])
    xw = lerp(x_c, x_p, mu_lora[l, 0])
    xa = lerp(x_c, x_p, mu_lora[l, 1])
    xg = lerp(x_c, x_p, mu_lora[l, 2])
    w_log = -jax.nn.softplus(-(w0[l] + jnp.tanh(xw @ w1[l]) @ w2[l])) - 0.5
    decay = jnp.exp(-jnp.exp(w_log.astype(jnp.float32)))
    if v_first is None:
        v_first = v
    else:
        xv = lerp(x_c, x_p, mu_vm[l - 1])
        v = v + (v_first - v) * jax.nn.sigmoid(v0[l - 1] + (xv @ v1[l - 1]) @ v2[l - 1])
    a = jax.nn.sigmoid(a0[l] + (xa @ a1[l]) @ a2[l])
    g = jax.nn.sigmoid(xg @ g1[l]) @ g2[l]
    heads = lambda t: t.reshape(B, T, H_B, HEAD_B)
    kk = heads(k * k_k[l]).astype(jnp.float32)
    kk = kk / jnp.maximum(jnp.sqrt(jnp.sum(kk * kk, axis=-1, keepdims=True)), 1e-12)
    k = k * (1.0 + (a - 1.0) * k_a[l])
    rh, kh, vh, ah = heads(r), heads(k), heads(v), heads(a).astype(jnp.float32)
    y, S = wkv7_scan(rh, heads(decay), kh, vh, -kk, kk * ah, S0)
    mu = jnp.mean(y, axis=-1, keepdims=True)
    var = jnp.mean(jnp.square(y - mu), axis=-1, keepdims=True)
    yn = (y - mu) * lax.rsqrt(var + EPS_GN)
    yn = (yn * ln_x_w[l].reshape(H_B, HEAD_B).astype(jnp.float32)
          + ln_x_b[l].reshape(H_B, HEAD_B).astype(jnp.float32))
    bonus = jnp.sum(rh.astype(jnp.float32) * kh.astype(jnp.float32)
                    * r_k[l].astype(jnp.float32), axis=-1, keepdims=True) * vh.astype(jnp.float32)
    out = (yn + bonus).reshape(B, T, D_B).astype(zb.dtype) * g
    return out, S.astype(S0.dtype), zb[:, -1, :], v_first


def trunk(x, c, shift0, wkv0, chunk_len, keep_chunk_rows, params):
    (w_ada, b_ada, g_pre_mix, g_post_mix, g_pre_ffn, g_post_ffn, w_in, w_out,
     ln_v_w, ln_v_b, w_spatial, b_spatial, mu_rkv, mu_lora, mu_vm, w0, w1, w2,
     a0, a1, a2, v0, v1, v2, g1, g2, k_k, k_a, r_k, ln_x_w, ln_x_b, w_up, w_down) = params
    v_first = None
    new_wkv, new_shift, chunk_rows = [], [], []
    c_act = jax.nn.silu(c)
    for l in range(DEPTH):
        mod = c_act @ w_ada[l] + b_ada[l]
        sh1, sc1, gt1, sh2, sc2, gt2 = [m[:, None, :] for m in jnp.split(mod, N_MOD, axis=-1)]
        h = rms_norm(x, g_pre_mix[l]) * (1.0 + sc1) + sh1
        z = h @ w_in[l]
        za, zb = z[..., :P_A], z[..., P_A:]
        u, va = jnp.split(jax.nn.gelu(za, approximate=False), 2, axis=-1)
        va = layer_norm(va, ln_v_w[l], ln_v_b[l])
        out_a = u * chunk_spatial_mix(va, w_spatial[l], b_spatial[l], chunk_len)
        out_b, S, sh_new, v_first = rwkv7_mix(
            zb, shift0[l], wkv0[l], v_first, l, mu_rkv, mu_lora, mu_vm, w0, w1, w2,
            a0, a1, a2, v0, v1, v2, g1, g2, k_k, k_a, r_k, ln_x_w, ln_x_b)
        y = jnp.concatenate([out_a, out_b], axis=-1) @ w_out[l]
        x = x + gt1 * rms_norm(y, g_post_mix[l])
        h = rms_norm(x, g_pre_ffn[l]) * (1.0 + sc2) + sh2
        f = jnp.square(jax.nn.relu(h @ w_up[l])) @ w_down[l]
        x = x + gt2 * rms_norm(f, g_post_ffn[l])
        new_wkv.append(S)
        new_shift.append(sh_new)
        if keep_chunk_rows:
            chunk_rows.append(va)
    rows = jnp.stack(chunk_rows) if keep_chunk_rows else None
    return x, jnp.stack(new_wkv), jnp.stack(new_shift), rows


def setup_inputs(seed: int = 0) -> dict:
    key = jax.random.key(seed)
    ks = jax.random.split(key, 40)
    f32 = jnp.float32
    nrm = lambda k, shape, s: jax.random.normal(k, shape, f32) * s
    uni = lambda k, shape, lo, hi: jax.random.uniform(k, shape, f32, lo, hi)
    gain = lambda k, shape: 1.0 + 0.05 * jax.random.normal(k, shape, f32)
    return {
        "x_prompt": nrm(ks[0], (BATCH, SEQ, D_MODEL), 1.0),
        "x_sample": nrm(ks[1], (DEC_BATCH, DEC_SEQ, D_MODEL), 1.0),
        "state_wkv": nrm(ks[2], (DEPTH, DEC_BATCH, H_B, HEAD_B, HEAD_B), 0.5),
        "state_shift": nrm(ks[3], (DEPTH, DEC_BATCH, P_B), 1.0),
        "c_prompt": nrm(ks[4], (BATCH, D_MODEL), 1.0),
        "c_sample": nrm(ks[5], (DEC_BATCH, D_MODEL), 1.0),
        "w_ada": nrm(ks[6], (DEPTH, D_MODEL, N_MOD * D_MODEL), 0.5 * D_MODEL ** -0.5),
        "b_ada": nrm(ks[7], (DEPTH, N_MOD * D_MODEL), 0.01),
        "g_pre_mix": gain(ks[8], (DEPTH, D_MODEL)),
        "g_post_mix": gain(ks[9], (DEPTH, D_MODEL)),
        "g_pre_ffn": gain(ks[10], (DEPTH, D_MODEL)),
        "g_post_ffn": gain(ks[11], (DEPTH, D_MODEL)),
        "w_in": nrm(ks[12], (DEPTH, D_MODEL, P_IN), D_MODEL ** -0.5),
        "w_out": nrm(ks[13], (DEPTH, D_MODEL, D_MODEL), D_MODEL ** -0.5),
        "ln_v_w": gain(ks[14], (DEPTH, D_A)),
        "ln_v_b": nrm(ks[15], (DEPTH, D_A), 0.01),
        "w_spatial": nrm(ks[16], (DEPTH, H_A, CHUNK, CHUNK), CHUNK ** -0.5),
        "b_spatial": 1.0 + nrm(ks[17], (DEPTH, H_A, CHUNK), 0.1),
        "mu_rkv": uni(ks[18], (DEPTH, 3, D_B), 0.0, 1.0),
        "mu_lora": uni(ks[19], (DEPTH, 3, D_B), 0.0, 1.0),
        "mu_vm": uni(ks[20], (DEPTH - 1, D_B), 0.0, 1.0),
        "w0": uni(ks[21], (DEPTH, D_B), -5.5, -0.5),
        "w1": nrm(ks[22], (DEPTH, D_B, LORA_W), D_B ** -0.5),
        "w2": nrm(ks[23], (DEPTH, LORA_W, D_B), 0.1 * LORA_W ** -0.5),
        "a0": nrm(ks[24], (DEPTH, D_B), 0.5),
        "a1": nrm(ks[25], (DEPTH, D_B, LORA_A), D_B ** -0.5),
        "a2": nrm(ks[26], (DEPTH, LORA_A, D_B), 0.5 * LORA_A ** -0.5),
        "v0": nrm(ks[27], (DEPTH - 1, D_B), 0.5),
        "v1": nrm(ks[28], (DEPTH - 1, D_B, LORA_V), D_B ** -0.5),
        "v2": nrm(ks[29], (DEPTH - 1, LORA_V, D_B), 0.5 * LORA_V ** -0.5),
        "g1": nrm(ks[30], (DEPTH, D_B, LORA_G), D_B ** -0.5),
        "g2": nrm(ks[31], (DEPTH, LORA_G, D_B), LORA_G ** -0.5),
        "k_k": 0.85 + nrm(ks[32], (DEPTH, D_B), 0.05),
        "k_a": gain(ks[33], (DEPTH, D_B)),
        "r_k": nrm(ks[34], (DEPTH, H_B, HEAD_B), 0.1),
        "ln_x_w": gain(ks[35], (DEPTH, D_B)),
        "ln_x_b": nrm(ks[36], (DEPTH, D_B), 0.01),
        "w_up": nrm(ks[37], (DEPTH, D_MODEL, D_FF), D_MODEL ** -0.5),
        "w_down": nrm(ks[38], (DEPTH, D_FF, D_MODEL), D_FF ** -0.5),
    }


def reference(x_prompt, x_sample, state_wkv, state_shift, c_prompt, c_sample,
              w_ada, b_ada, g_pre_mix, g_post_mix, g_pre_ffn, g_post_ffn,
              w_in, w_out, ln_v_w, ln_v_b, w_spatial, b_spatial,
              mu_rkv, mu_lora, mu_vm, w0, w1, w2, a0, a1, a2, v0, v1, v2,
              g1, g2, k_k, k_a, r_k, ln_x_w, ln_x_b, w_up, w_down):
    params = (w_ada, b_ada, g_pre_mix, g_post_mix, g_pre_ffn, g_post_ffn, w_in, w_out,
              ln_v_w, ln_v_b, w_spatial, b_spatial, mu_rkv, mu_lora, mu_vm, w0, w1, w2,
              a0, a1, a2, v0, v1, v2, g1, g2, k_k, k_a, r_k, ln_x_w, ln_x_b, w_up, w_down)
    n_prompt = x_prompt.shape[0]
    shift0 = jnp.zeros((DEPTH, n_prompt, P_B), x_prompt.dtype)
    wkv0 = jnp.zeros((DEPTH, n_prompt, H_B, HEAD_B, HEAD_B), state_wkv.dtype)
    y_prompt, wkv_p, shift_p, _ = trunk(x_prompt, c_prompt, shift0, wkv0, CHUNK, False, params)
    y_sample, wkv_s, shift_s, chunk_v_s = trunk(x_sample, c_sample, state_shift, state_wkv,
                                                x_sample.shape[1], True, params)
    return (y_prompt, y_sample, wkv_p, shift_p, wkv_s, shift_s, chunk_v_s)
```

```python
import functools

import jax
import jax.numpy as jnp
from jax import lax
from jax.experimental import pallas as pl
from jax.experimental.pallas import tpu as pltpu

F32 = jnp.float32
BF16 = jnp.bfloat16

H_A = 8
CHUNK = 128
HEAD_B = 64
N_MOD = 6
EPS_RMS = 1e-6
EPS_LN = 1e-5
EPS_GN = 64e-5

V7X_LANES = 128
V7X_VMEM_BYTES = 64 * 1024 * 1024
VMEM_LIMIT_CAP = 56 * 1024 * 1024


def _vmem_limit(block_bytes, scratch_bytes=0, temp_bytes=0):
  est = 2 * block_bytes + scratch_bytes + temp_bytes + (2 << 20)
  return int(min(max(est, 16 << 20), VMEM_LIMIT_CAP))


def _tile(dim, target, align=V7X_LANES):
  if dim <= target:
    return dim
  t = (target // align) * align
  while t >= align:
    if dim % t == 0:
      return t
    t -= align
  return dim


def _nbytes(shape, dtype):
  n = 1
  for s in shape:
    n *= s
  return n * jnp.dtype(dtype).itemsize


def _relu2(x):
  r = jnp.maximum(x, 0.0)
  return r * r


def _mm_kernel(a_ref, b_ref, o_ref, *scratch, nk, act):
  if nk == 1:
    acc = jnp.dot(a_ref[...], b_ref[...], preferred_element_type=F32)
    o_ref[...] = (act(acc) if act else acc).astype(o_ref.dtype)
    return
  (acc_ref,) = scratch
  k = pl.program_id(2)

  @pl.when(k == 0)
  def _():
    acc_ref[...] = jnp.zeros_like(acc_ref)

  acc_ref[...] += jnp.dot(a_ref[...], b_ref[...], preferred_element_type=F32)

  @pl.when(k == nk - 1)
  def _():
    acc = acc_ref[...]
    o_ref[...] = (act(acc) if act else acc).astype(o_ref.dtype)


def _matmul(a, b, out_dtype, *, act=None, tm=1024, tn=1024, tk=2048, name="matmul"):
  m, kdim = a.shape
  _, n = b.shape
  tm, tn, tk = _tile(m, tm, 8), _tile(n, tn), _tile(kdim, tk)
  nk = kdim // tk
  blocks = (_nbytes((tm, tk), a.dtype) + _nbytes((tk, tn), b.dtype)
            + _nbytes((tm, tn), out_dtype))
  acc_bytes = _nbytes((tm, tn), F32)
  return pl.pallas_call(
      functools.partial(_mm_kernel, nk=nk, act=act),
      out_shape=jax.ShapeDtypeStruct((m, n), out_dtype),
      grid=(m // tm, n // tn, nk),
      in_specs=[pl.BlockSpec((tm, tk), lambda i, j, k: (i, k)),
                pl.BlockSpec((tk, tn), lambda i, j, k: (k, j))],
      out_specs=pl.BlockSpec((tm, tn), lambda i, j, k: (i, j)),
      scratch_shapes=[pltpu.VMEM((tm, tn), F32)] if nk > 1 else [],
      compiler_params=pltpu.CompilerParams(
          dimension_semantics=("parallel", "parallel", "arbitrary"),
          vmem_limit_bytes=_vmem_limit(blocks, acc_bytes, 2 * acc_bytes)),
      name=name,
  )(a, b)


def _ada_kernel(c_ref, w_ref, b_ref, o_ref, acc_ref, *, nk):
  k = pl.program_id(2)

  @pl.when(k == 0)
  def _():
    acc_ref[...] = jnp.zeros_like(acc_ref)

  c = c_ref[...]
  c_act = (c * jax.nn.sigmoid(c)).astype(BF16)
  acc_ref[...] += jnp.dot(c_act, w_ref[...].astype(BF16), preferred_element_type=F32)

  @pl.when(k == nk - 1)
  def _():
    o_ref[...] = acc_ref[...] + b_ref[...]


def _ada_all(c_all, w_ada, b_ada):
  depth, d, n = w_ada.shape
  mp = c_all.shape[0]
  tk, tn = _tile(d, 1024), _tile(n, 2048)
  nk = d // tk
  blocks = (_nbytes((mp, tk), F32) + _nbytes((tk, tn), F32) + _nbytes((1, tn), F32)
            + _nbytes((mp, tn), F32))
  return pl.pallas_call(
      functools.partial(_ada_kernel, nk=nk),
      out_shape=jax.ShapeDtypeStruct((depth, mp, n), F32),
      grid=(depth, n // tn, nk),
      in_specs=[pl.BlockSpec((mp, tk), lambda l, j, k: (0, k)),
                pl.BlockSpec((None, tk, tn), lambda l, j, k: (l, k, j)),
                pl.BlockSpec((None, 1, tn), lambda l, j, k: (l, 0, j))],
      out_specs=pl.BlockSpec((None, mp, tn), lambda l, j, k: (l, 0, j)),
      scratch_shapes=[pltpu.VMEM((mp, tn), F32)],
      compiler_params=pltpu.CompilerParams(
          dimension_semantics=("parallel", "parallel", "arbitrary"),
          vmem_limit_bytes=_vmem_limit(blocks, _nbytes((mp, tn), F32),
                                       _nbytes((tk, tn), BF16))),
      name="ada_mod",
  )(c_all, w_ada, b_ada.reshape(depth, 1, n))


class _Pass:
  def __init__(self, name, b, t, d, time_major):
    self.name, self.b, self.t, self.d, self.time_major = name, b, t, d, time_major
    self.m = b * t
    self.tm = b if time_major else _tile(t, CHUNK, 8)
    self.n_tiles = self.m // self.tm
    self.tiles_per_seq = 1 if time_major else t // self.tm

  def row_spec(self, width, col=0):
    return pl.BlockSpec((self.tm, width), lambda i, c=col: (i, c))

  def mod_spec(self, which):
    d = self.d
    if self.time_major:
      return pl.BlockSpec((self.tm, d), lambda i, c=which: (0, c))
    tps = self.tiles_per_seq
    return pl.BlockSpec((None, 1, d), lambda i, c=which: (i // tps, 0, c))

  def mod_rows(self):
    return self.tm if self.time_major else 1


def _vec_spec(width):
  return pl.BlockSpec((1, width), lambda i: (0, 0))


def _full_spec(shape):
  nd = len(shape)
  return pl.BlockSpec(shape, lambda i, nd=nd: (0,) * nd)


def _row_params(p, blocks, temps=0):
  return pltpu.CompilerParams(
      dimension_semantics=("parallel",),
      vmem_limit_bytes=_vmem_limit(blocks, 0, temps))


def _rms(x, g):
  ms = jnp.mean(x * x, axis=-1, keepdims=True)
  return x * lax.rsqrt(ms + EPS_RMS) * g


def _prenorm_kernel(x_ref, g_ref, sc_ref, sh_ref, h_ref):
  h = _rms(x_ref[...], g_ref[...]) * (1.0 + sc_ref[...]) + sh_ref[...]
  h_ref[...] = h.astype(h_ref.dtype)


def _resid_kernel(x_ref, y_ref, gt_ref, gpost_ref, xo_ref):
  xo_ref[...] = x_ref[...] + gt_ref[...] * _rms(y_ref[...], gpost_ref[...])


def _resid_prenorm_kernel(x_ref, y_ref, gt_ref, gpost_ref, gpre_ref, sc_ref, sh_ref,
                          xo_ref, h_ref):
  xo = x_ref[...] + gt_ref[...] * _rms(y_ref[...], gpost_ref[...])
  xo_ref[...] = xo
  h = _rms(xo, gpre_ref[...]) * (1.0 + sc_ref[...]) + sh_ref[...]
  h_ref[...] = h.astype(h_ref.dtype)


def _prenorm(p, x, mod, g_pre, sc_idx, sh_idx):
  d = p.d
  blocks = _nbytes((p.tm, d), F32) + _nbytes((p.tm, d), BF16) + 3 * _nbytes((p.mod_rows(), d), F32)
  return pl.pallas_call(
      _prenorm_kernel,
      out_shape=jax.ShapeDtypeStruct((p.m, d), BF16),
      grid=(p.n_tiles,),
      in_specs=[p.row_spec(d), _vec_spec(d), p.mod_spec(sc_idx), p.mod_spec(sh_idx)],
      out_specs=p.row_spec(d),
      compiler_params=_row_params(p, blocks, 3 * _nbytes((p.tm, d), F32)),
      name=f"prenorm_{p.name}",
  )(x, g_pre.reshape(1, d), mod, mod)


def _resid(p, x, y, mod, gt_idx, g_post):
  d = p.d
  blocks = 3 * _nbytes((p.tm, d), F32) + 2 * _nbytes((p.mod_rows(), d), F32)
  return pl.pallas_call(
      _resid_kernel,
      out_shape=jax.ShapeDtypeStruct((p.m, d), F32),
      grid=(p.n_tiles,),
      in_specs=[p.row_spec(d), p.row_spec(d), p.mod_spec(gt_idx), _vec_spec(d)],
      out_specs=p.row_spec(d),
      compiler_params=_row_params(p, blocks, 3 * _nbytes((p.tm, d), F32)),
      name=f"resid_{p.name}",
  )(x, y, mod, g_post.reshape(1, d))


def _resid_prenorm(p, x, y, mod, gt_idx, g_post, mod_pre, g_pre, sc_idx, sh_idx):
  d = p.d
  blocks = (3 * _nbytes((p.tm, d), F32) + _nbytes((p.tm, d), BF16)
            + 5 * _nbytes((p.mod_rows(), d), F32))
  return pl.pallas_call(
      _resid_prenorm_kernel,
      out_shape=(jax.ShapeDtypeStruct((p.m, d), F32), jax.ShapeDtypeStruct((p.m, d), BF16)),
      grid=(p.n_tiles,),
      in_specs=[p.row_spec(d), p.row_spec(d), p.mod_spec(gt_idx), _vec_spec(d), _vec_spec(d),
                p.mod_spec(sc_idx), p.mod_spec(sh_idx)],
      out_specs=(p.row_spec(d), p.row_spec(d)),
      compiler_params=_row_params(p, blocks, 4 * _nbytes((p.tm, d), F32)),
      name=f"resid_prenorm_{p.name}",
  )(x, y, mod, g_post.reshape(1, d), g_pre.reshape(1, d), mod_pre, mod_pre)


def _gelu_exact(x):
  return 0.5 * x * (1.0 + lax.erf(x * (0.5 ** 0.5)))


def _layer_norm(x, w, b):
  mu = jnp.mean(x, axis=-1, keepdims=True)
  xc = x - mu
  var = jnp.mean(xc * xc, axis=-1, keepdims=True)
  return xc * lax.rsqrt(var + EPS_LN) * w + b


def _sgu_prompt_kernel(z_ref, lnw_ref, lnb_ref, ws_ref, bs_ref, oa_ref, *, d_a):
  dh = d_a // H_A
  ga = _gelu_exact(z_ref[...])
  u = ga[:, :d_a]
  van = _layer_norm(ga[:, d_a:], lnw_ref[...], lnb_ref[...]).astype(BF16)
  row = lax.broadcasted_iota(jnp.int32, (CHUNK, CHUNK), 0)
  col = lax.broadcasted_iota(jnp.int32, (CHUNK, CHUNK), 1)
  causal = row >= col
  for h in range(H_A):
    sl = slice(h * dh, (h + 1) * dh)
    w_h = jnp.where(causal, ws_ref[h], 0.0).astype(BF16)
    s_h = jnp.dot(w_h, van[:, sl], preferred_element_type=F32) + bs_ref[:, h:h + 1]
    oa_ref[:, sl] = (u[:, sl] * s_h).astype(oa_ref.dtype)


def _sgu_prompt(p, z, ln_w, ln_b, w_sp, b_sp, d_a):
  assert p.tm == CHUNK
  blocks = (_nbytes((CHUNK, 2 * d_a), F32) + _nbytes((H_A, CHUNK, CHUNK), F32)
            + _nbytes((CHUNK, d_a), BF16))
  return pl.pallas_call(
      functools.partial(_sgu_prompt_kernel, d_a=d_a),
      out_shape=jax.ShapeDtypeStruct((p.m, d_a), BF16),
      grid=(p.n_tiles,),
      in_specs=[p.row_spec(2 * d_a), _vec_spec(d_a), _vec_spec(d_a),
                _full_spec((H_A, CHUNK, CHUNK)), _full_spec((CHUNK, H_A))],
      out_specs=p.row_spec(d_a),
      compiler_params=_row_params(p, blocks, 4 * _nbytes((CHUNK, 2 * d_a), F32)),
      name=f"sgu_{p.name}",
  )(z, ln_w.reshape(1, d_a), ln_b.reshape(1, d_a), w_sp[:, :CHUNK, :CHUNK],
    b_sp[:, :CHUNK].T)


def _sgu_sample_kernel(z_ref, lnw_ref, lnb_ref, wsv_ref, bsv_ref, oa_ref, vn_ref, *, d_a, t, b):
  ga = _gelu_exact(z_ref[...])
  van = _layer_norm(ga[:, d_a:], lnw_ref[...], lnb_ref[...])
  vn_ref[...] = van
  for i in range(t):
    s = bsv_ref[i:i + 1, :]
    for j in range(i + 1):
      s = s + wsv_ref[i * t + j:i * t + j + 1, :] * van[j * b:(j + 1) * b, :]
    oa_ref[i * b:(i + 1) * b, :] = (ga[i * b:(i + 1) * b, :d_a] * s).astype(oa_ref.dtype)


def _sgu_sample(p, z, ln_w, ln_b, w_sp, b_sp, d_a):
  t, b = p.t, p.b
  dh = d_a // H_A
  wsv = jnp.repeat(w_sp[:, :t, :t].transpose(1, 2, 0).reshape(t * t, H_A), dh, axis=1)
  bsv = jnp.repeat(b_sp[:, :t].T, dh, axis=1)
  blocks = (_nbytes((p.m, 2 * d_a), F32) + _nbytes((p.m, d_a), BF16) + _nbytes((p.m, d_a), F32))
  return pl.pallas_call(
      functools.partial(_sgu_sample_kernel, d_a=d_a, t=t, b=b),
      out_shape=(jax.ShapeDtypeStruct((p.m, d_a), BF16), jax.ShapeDtypeStruct((p.m, d_a), F32)),
      grid=(1,),
      in_specs=[pl.BlockSpec((p.m, 2 * d_a), lambda i: (0, 0)), _vec_spec(d_a), _vec_spec(d_a),
                _full_spec((t * t, d_a)), _full_spec((t, d_a))],
      out_specs=(_full_spec((p.m, d_a)), _full_spec((p.m, d_a))),
      compiler_params=pltpu.CompilerParams(
          dimension_semantics=("arbitrary",),
          vmem_limit_bytes=_vmem_limit(blocks, 0, 3 * _nbytes((p.m, 2 * d_a), F32))),
      name=f"sgu_{p.name}",
  )(z, ln_w.reshape(1, d_a), ln_b.reshape(1, d_a), wsv, bsv)


def _lerp(cur, delta, mu):
  return cur + delta * mu


def _rwkv_prep_kernel(*refs, roll_shift, has_vfirst):
  (rc_ref, kc_ref, vc_ref, xc_ref, rp_ref, kp_ref, vp_ref, xp_ref, mu_ref, bias_ref,
   w1_ref, w2_ref, a1_ref, a2_ref, g1_ref, g2_ref) = refs[:16]
  pos = 16
  if has_vfirst:
    v1_ref, v2_ref, vf_ref = refs[pos:pos + 3]
    pos += 3
  r_ref, w_ref, k_ref, v_ref, ag_ref, g_ref = refs[pos:pos + 6]

  def delta(cur_ref, prev_ref):
    cur = cur_ref[...]
    if roll_shift:
      rolled = pltpu.roll(cur, 1, axis=0)
      row = lax.broadcasted_iota(jnp.int32, cur.shape, 0)
      prev = jnp.where(row == 0, prev_ref[...], rolled)
    else:
      prev = prev_ref[...]
    return cur, prev - cur

  def lora(x, m1_ref, m2_ref, mid=None):
    hcol = jnp.dot(x.astype(BF16), m1_ref[...], preferred_element_type=F32)
    if mid is not None:
      hcol = mid(hcol)
    return jnp.dot(hcol.astype(BF16), m2_ref[...], preferred_element_type=F32)

  mu = lambda i: mu_ref[i:i + 1, :]
  bias = lambda i: bias_ref[i:i + 1, :]

  cur, dl = delta(rc_ref, rp_ref)
  r_ref[...] = _lerp(cur, dl, mu(0))
  cur, dl = delta(kc_ref, kp_ref)
  k_ref[...] = _lerp(cur, dl, mu(1))
  cur, dl = delta(vc_ref, vp_ref)
  v = _lerp(cur, dl, mu(2))
  xcur, xdl = delta(xc_ref, xp_ref)

  u = bias(0) + lora(_lerp(xcur, xdl, mu(3)), w1_ref, w2_ref, jnp.tanh)
  w_log = -jax.nn.softplus(-u) - 0.5
  w_ref[...] = jnp.exp(-jnp.exp(w_log))
  if has_vfirst:
    gate = jax.nn.sigmoid(bias(2) + lora(_lerp(xcur, xdl, mu(6)), v1_ref, v2_ref))
    v = v + (vf_ref[...] - v) * gate
  v_ref[...] = v
  ag_ref[...] = jax.nn.sigmoid(bias(1) + lora(_lerp(xcur, xdl, mu(4)), a1_ref, a2_ref))
  g_ref[...] = lora(_lerp(xcur, xdl, mu(5)), g1_ref, g2_ref, jax.nn.sigmoid)


def _pad_lora(m1, m2):
  rank = m1.shape[1]
  rp = -(-rank // V7X_LANES) * V7X_LANES
  m1 = jnp.pad(m1, ((0, 0), (0, rp - rank))).astype(BF16)
  m2 = jnp.pad(m2, ((0, rp - rank), (0, 0))).astype(BF16)
  return m1, m2


def _rwkv_prep(p, z, prev, mu7, bias3, loras, v_first, d_b, col0):
  has_vfirst = v_first is not None
  cur_specs = [p.row_spec(d_b, col0 + s) for s in range(4)]
  if p.time_major:
    prev_specs = [p.row_spec(d_b, s) for s in range(4)]
  else:
    prev_specs = [pl.BlockSpec((None, 1, d_b), lambda i, c=s: (i, 0, c)) for s in range(4)]
  w1, w2, a1, a2, g1, g2, v1, v2 = loras
  ins = [z] * 4 + [prev] * 4 + [mu7, bias3, w1, w2, a1, a2, g1, g2]
  specs = cur_specs + prev_specs + [_full_spec(mu7.shape), _full_spec(bias3.shape)]
  specs += [_full_spec(m.shape) for m in (w1, w2, a1, a2, g1, g2)]
  if has_vfirst:
    ins += [v1, v2, v_first]
    specs += [_full_spec(v1.shape), _full_spec(v2.shape), p.row_spec(d_b)]
  tile = _nbytes((p.tm, d_b), F32)
  blocks = (15 if p.time_major else 11) * tile + 8 * _nbytes(w1.shape, BF16)
  out = jax.ShapeDtypeStruct((p.m, d_b), F32)
  return pl.pallas_call(
      functools.partial(_rwkv_prep_kernel, roll_shift=not p.time_major, has_vfirst=has_vfirst),
      out_shape=(out,) * 6,
      grid=(p.n_tiles,),
      in_specs=specs,
      out_specs=(p.row_spec(d_b),) * 6,
      compiler_params=_row_params(p, blocks, 10 * tile),
      name=f"rwkv_prep_{p.name}",
  )(*ins)


def _wkv_kernel(r_ref, w_ref, k_ref, v_ref, ag_ref, kk_ref, ka_ref, rk_ref, lnw_ref, lnb_ref,
                s0_ref, o_ref, s_ref, a_scr, b_scr, km_scr, y_scr, *, tb, n):
  step_blk = pl.program_id(1)

  @pl.when(step_blk == 0)
  def _():
    s_ref[...] = s0_ref[...]

  k_raw = k_ref[...]
  ag = ag_ref[...]
  kk = k_raw * kk_ref[...]
  nrm = jnp.sqrt(jnp.sum(kk * kk, axis=1, keepdims=True))
  kk = kk / jnp.maximum(nrm, 1e-12)
  a_scr[0:tb] = -kk
  a_scr[tb] = jnp.zeros(a_scr.shape[1:], F32)
  b_scr[...] = kk * ag
  k_mod = k_raw * (1.0 + (ag - 1.0) * ka_ref[...])
  km_scr[...] = k_mod

  def matvec(vec_ref_row):
    acc = jnp.zeros(s_ref.shape[1:], F32)
    for j in range(n):
      acc = acc + s_ref[j] * vec_ref_row(j)
    return acc

  sa0 = matvec(lambda j: a_scr[0, j:j + 1, :])

  def step(t, sa):
    v_t = v_ref[t]
    y = jnp.zeros_like(sa)
    sa_next = jnp.zeros_like(sa)
    for j in range(n):
      s_new = (s_ref[j] * w_ref[t, j:j + 1, :] + sa * b_scr[t, j:j + 1, :]
               + v_t * km_scr[t, j:j + 1, :])
      s_ref[j] = s_new
      y = y + s_new * r_ref[t, j:j + 1, :]
      sa_next = sa_next + s_new * a_scr[t + 1, j:j + 1, :]
    y_scr[t] = y
    return sa_next

  lax.fori_loop(0, tb, step, sa0)

  y = y_scr[...]
  mu = jnp.mean(y, axis=1, keepdims=True)
  yc = y - mu
  var = jnp.mean(yc * yc, axis=1, keepdims=True)
  yn = yc * lax.rsqrt(var + EPS_GN) * lnw_ref[...] + lnb_ref[...]
  rk = jnp.sum(r_ref[...] * k_mod * rk_ref[...], axis=1, keepdims=True)
  o_ref[...] = yn + rk * v_ref[...]


def _wkv(seq, params, s0, *, tb, name):
  t, g, n, l = seq[0].shape
  tb = _tile(t, tb, 1)
  seq_spec = pl.BlockSpec((tb, None, n, l), lambda gi, ti: (ti, gi, 0, 0))
  par_spec = pl.BlockSpec((None, n, l), lambda gi, ti: (gi, 0, 0))
  st_spec = pl.BlockSpec((None, n, n, l), lambda gi, ti: (gi, 0, 0, 0))
  blk = _nbytes((tb, n, l), F32)
  st = _nbytes((n, n, l), F32)
  return pl.pallas_call(
      functools.partial(_wkv_kernel, tb=tb, n=n),
      out_shape=(jax.ShapeDtypeStruct((t, g, n, l), F32), jax.ShapeDtypeStruct((g, n, n, l), F32)),
      grid=(g, t // tb),
      in_specs=[seq_spec] * 5 + [par_spec] * 5 + [st_spec],
      out_specs=(seq_spec, st_spec),
      scratch_shapes=[pltpu.VMEM((tb + 1, n, l), F32), pltpu.VMEM((tb, n, l), F32),
                      pltpu.VMEM((tb, n, l), F32), pltpu.VMEM((tb, n, l), F32)],
      compiler_params=pltpu.CompilerParams(
          dimension_semantics=("parallel", "arbitrary"),
          vmem_limit_bytes=_vmem_limit(6 * blk + 2 * st + 5 * _nbytes((n, l), F32),
                                       4 * blk + _nbytes((n, l), F32), 6 * blk)),
      name=name,
  )(*seq, *params, s0)


def _cat_kernel(oa_ref, ob_ref, g_ref, o_ref, *, d_a):
  o_ref[:, :d_a] = oa_ref[...]
  o_ref[:, d_a:] = (ob_ref[...] * g_ref[...]).astype(o_ref.dtype)


def _cat(p, oa, ob, g, d_a, d_b):
  d = d_a + d_b
  blocks = _nbytes((p.tm, d_a), BF16) + 2 * _nbytes((p.tm, d_b), F32) + _nbytes((p.tm, d), BF16)
  return pl.pallas_call(
      functools.partial(_cat_kernel, d_a=d_a),
      out_shape=jax.ShapeDtypeStruct((p.m, d), BF16),
      grid=(p.n_tiles,),
      in_specs=[p.row_spec(d_a), p.row_spec(d_b), p.row_spec(d_b)],
      out_specs=p.row_spec(d),
      compiler_params=_row_params(p, blocks, _nbytes((p.tm, d_b), F32)),
      name=f"cat_{p.name}",
  )(oa, ob, g)


def _to_scan(p, x, h_b):
  if p.time_major:
    return x.reshape(p.t, p.b, h_b, HEAD_B).transpose(0, 2, 3, 1)
  return x.reshape(p.b, p.t, h_b, HEAD_B).transpose(1, 3, 0, 2).reshape(p.t, 1, HEAD_B, p.b * h_b)


def _from_scan(p, y, h_b):
  if p.time_major:
    return y.transpose(0, 3, 1, 2).reshape(p.m, h_b * HEAD_B)
  return (y.reshape(p.t, HEAD_B, p.b, h_b).transpose(2, 0, 3, 1).reshape(p.m, h_b * HEAD_B))


def _scan_param(p, v, h_b):
  vh = v.reshape(h_b, HEAD_B)
  if p.time_major:
    return jnp.broadcast_to(vh[:, :, None], (h_b, HEAD_B, p.b))
  return jnp.tile(vh.T, (1, p.b))[None]


def _state_to_scan(p, s):
  if p.time_major:
    return s.transpose(1, 3, 2, 0)
  b, h_b = s.shape[:2]
  return s.transpose(3, 2, 0, 1).reshape(1, HEAD_B, HEAD_B, b * h_b)


def _state_from_scan(p, s, h_b):
  if p.time_major:
    return s.transpose(3, 0, 2, 1)
  return s.reshape(HEAD_B, HEAD_B, p.b, h_b).transpose(2, 3, 1, 0)


def _trunk(p, x, mod_l, shift0, wkv0, weights, keep_chunk_rows):
  (g_pre_mix, g_post_mix, g_pre_ffn, g_post_ffn, w_in, w_out, ln_v_w, ln_v_b, w_spatial,
   b_spatial, mu_rkv, mu_lora, mu_vm, w0, a0, v0, loras, k_k, k_a, r_k, ln_x_w, ln_x_b,
   w_up, w_down) = weights
  depth = w_in.shape[0]
  d = p.d
  d_a = d // 2
  d_b = d - d_a
  h_b = d_b // HEAD_B
  p_a = 2 * d_a
  assert p_a % d_b == 0
  mm_tm = 1024 if not p.time_major else 512
  new_wkv, new_shift, chunk_rows = [], [], []
  v_first = None
  h = _prenorm(p, x, mod_l[0], g_pre_mix[0], 1, 0)
  for l in range(depth):
    mod = mod_l[l]
    z = _matmul(h, w_in[l], F32, tm=mm_tm, name=f"mm_in_{p.name}")
    if p.time_major:
      out_a, va_n = _sgu_sample(p, z, ln_v_w[l], ln_v_b[l], w_spatial[l], b_spatial[l], d_a)
      if keep_chunk_rows:
        chunk_rows.append(va_n)
    else:
      out_a = _sgu_prompt(p, z, ln_v_w[l], ln_v_b[l], w_spatial[l], b_spatial[l], d_a)
    if p.time_major:
      zb = z[:, p_a:]
      prev = jnp.concatenate([shift0[l], zb[:-p.b]], axis=0)
      new_shift.append(zb[(p.t - 1) * p.b:])
    else:
      zl = z.reshape(p.b, p.tiles_per_seq, p.tm, z.shape[1])[:, :, p.tm - 1, p_a:]
      prev = jnp.concatenate([shift0[l][:, None, :], zl[:, :-1]], axis=1)
      prev = prev.reshape(p.n_tiles, 1, 4 * d_b)
      new_shift.append(zl[:, -1])
    mu7 = jnp.concatenate([mu_rkv[l], mu_lora[l], mu_vm[max(l - 1, 0)][None]], axis=0)
    bias3 = jnp.stack([w0[l], a0[l], v0[max(l - 1, 0)]])
    r, w, k, v, ag, g = _rwkv_prep(p, z, prev, mu7, bias3, loras[l], v_first, d_b, p_a // d_b)
    if l == 0:
      v_first = v
    seq = [_to_scan(p, t, h_b) for t in (r, w, k, v, ag)]
    par = [_scan_param(p, t, h_b) for t in (k_k[l], k_a[l], r_k[l].reshape(-1), ln_x_w[l], ln_x_b[l])]
    ob, s_new = _wkv(seq, par, _state_to_scan(p, wkv0[l]), tb=32, name=f"wkv_{p.name}")
    new_wkv.append(_state_from_scan(p, s_new, h_b))
    cat = _cat(p, out_a, _from_scan(p, ob, h_b), g, d_a, d_b)
    y = _matmul(cat, w_out[l], F32, tm=mm_tm, name=f"mm_out_{p.name}")
    x, h = _resid_prenorm(p, x, y, mod, 2, g_post_mix[l], mod, g_pre_ffn[l], 4, 3)
    f = _matmul(h, w_up[l], BF16, act=_relu2, tm=mm_tm, name=f"mm_up_{p.name}")
    f = _matmul(f, w_down[l], F32, tm=mm_tm, name=f"mm_down_{p.name}")
    if l + 1 < depth:
      x, h = _resid_prenorm(p, x, f, mod, 5, g_post_ffn[l], mod_l[l + 1], g_pre_mix[l + 1], 1, 0)
    else:
      x = _resid(p, x, f, mod, 5, g_post_ffn[l])
  rows = jnp.stack(chunk_rows) if keep_chunk_rows else None
  return x, jnp.stack(new_wkv), jnp.stack(new_shift), rows


def kernel(x_prompt, x_sample, state_wkv, state_shift, c_prompt, c_sample, w_ada, b_ada, g_pre_mix, g_post_mix, g_pre_ffn, g_post_ffn, w_in, w_out, ln_v_w, ln_v_b, w_spatial, b_spatial, mu_rkv, mu_lora, mu_vm, w0, w1, w2, a0, a1, a2, v0, v1, v2, g1, g2, k_k, k_a, r_k, ln_x_w, ln_x_b, w_up, w_down):
  bp, tp, d = x_prompt.shape
  bs, ts, _ = x_sample.shape
  depth = w_in.shape[0]
  d_b = d - d // 2
  h_b = d_b // HEAD_B

  n_c = bs + bp
  c_all = jnp.concatenate([c_sample, c_prompt, jnp.zeros((-n_c % 8, d), F32)], axis=0)
  mod_all = _ada_all(c_all, w_ada, b_ada)
  mod_s = [mod_all[l, :bs] for l in range(depth)]
  mod_p = [mod_all[l, bs:n_c].reshape(bp, 1, N_MOD * d) for l in range(depth)]

  loras = []
  for l in range(depth):
    lv = max(l - 1, 0)
    loras.append(_pad_lora(w1[l], w2[l]) + _pad_lora(a1[l], a2[l]) + _pad_lora(g1[l], g2[l])
                 + _pad_lora(v1[lv], v2[lv]))
  weights = (g_pre_mix, g_post_mix, g_pre_ffn, g_post_ffn, w_in.astype(BF16), w_out.astype(BF16),
             ln_v_w, ln_v_b, w_spatial, b_spatial, mu_rkv, mu_lora, mu_vm, w0, a0, v0, loras,
             k_k, k_a, r_k, ln_x_w, ln_x_b, w_up.astype(BF16), w_down.astype(BF16))

  pp = _Pass("prompt", bp, tp, d, time_major=False)
  shift0 = jnp.zeros((depth, bp, 4 * d_b), x_prompt.dtype)
  wkv0 = jnp.zeros((depth, bp, h_b, HEAD_B, HEAD_B), state_wkv.dtype)
  y_p, wkv_p, shift_p, _ = _trunk(pp, x_prompt.reshape(bp * tp, d), mod_p, shift0, wkv0,
                                  weights, False)

  ps = _Pass("sample", bs, ts, d, time_major=True)
  x_s = x_sample.transpose(1, 0, 2).reshape(ts * bs, d)
  y_s, wkv_s, shift_s, rows_s = _trunk(ps, x_s, mod_s, state_shift, state_wkv, weights, True)

  y_sample = y_s.reshape(ts, bs, d).transpose(1, 0, 2)
  chunk_v = rows_s.reshape(depth, ts, bs, d // 2).transpose(0, 2, 1, 3)
  return (y_p.reshape(bp, tp, d), y_sample, wkv_p, shift_p, wkv_s, shift_s, chunk_v)
```

```python
import functools

import jax
import jax.numpy as jnp
from jax import lax
from jax.experimental import pallas as pl
from jax.experimental.pallas import tpu as pltpu

F32 = jnp.float32
BF16 = jnp.bfloat16

H_A = 8
CHUNK = 128
HEAD_B = 64
N_MOD = 6
EPS_RMS = 1e-6
EPS_LN = 1e-5
EPS_GN = 64e-5

V7X_LANES = 128
V7X_SUBLANES = 8
VMEM_LIMIT_CAP = 56 * 1024 * 1024


def _vmem_limit(block_bytes, scratch_bytes=0, temp_bytes=0):
  est = 2 * block_bytes + scratch_bytes + temp_bytes + (2 << 20)
  return int(min(max(est, 16 << 20), VMEM_LIMIT_CAP))


def _tile(dim, target, align=V7X_LANES):
  if dim <= target:
    return dim
  t = (target // align) * align
  while t >= align:
    if dim % t == 0:
      return t
    t -= align
  return dim


def _nbytes(shape, dtype):
  n = 1
  for s in shape:
    n *= s
  return n * jnp.dtype(dtype).itemsize


def _relu2(x):
  r = jnp.maximum(x, 0.0)
  return r * r


def _mm_kernel(a_ref, b_ref, o_ref, *scratch, nk, act):
  if nk == 1:
    acc = jnp.dot(a_ref[...], b_ref[...], preferred_element_type=F32)
    o_ref[...] = (act(acc) if act else acc).astype(o_ref.dtype)
    return
  (acc_ref,) = scratch
  k = pl.program_id(2)

  @pl.when(k == 0)
  def _():
    acc_ref[...] = jnp.zeros_like(acc_ref)

  acc_ref[...] += jnp.dot(a_ref[...], b_ref[...], preferred_element_type=F32)

  @pl.when(k == nk - 1)
  def _():
    acc = acc_ref[...]
    o_ref[...] = (act(acc) if act else acc).astype(o_ref.dtype)


def _matmul(a, w, layer, out_dtype, *, act=None, tm=1024, tn=1024, tk=2048, name="matmul"):
  m, kdim = a.shape
  n = w.shape[2]
  tm, tn, tk = _tile(m, tm, V7X_SUBLANES), _tile(n, tn), _tile(kdim, tk)
  nk = kdim // tk
  blocks = (_nbytes((tm, tk), a.dtype) + _nbytes((tk, tn), w.dtype)
            + _nbytes((tm, tn), out_dtype))
  acc_bytes = _nbytes((tm, tn), F32)
  return pl.pallas_call(
      functools.partial(_mm_kernel, nk=nk, act=act),
      out_shape=jax.ShapeDtypeStruct((m, n), out_dtype),
      grid=(m // tm, n // tn, nk),
      in_specs=[pl.BlockSpec((tm, tk), lambda i, j, k: (i, k)),
                pl.BlockSpec((None, tk, tn), lambda i, j, k: (layer, k, j))],
      out_specs=pl.BlockSpec((tm, tn), lambda i, j, k: (i, j)),
      scratch_shapes=[pltpu.VMEM((tm, tn), F32)] if nk > 1 else [],
      compiler_params=pltpu.CompilerParams(
          dimension_semantics=("parallel", "parallel", "arbitrary"),
          vmem_limit_bytes=_vmem_limit(blocks, acc_bytes, 2 * acc_bytes)),
      name=name,
  )(a, w)


def _ada_kernel(c_ref, w_ref, b_ref, o_ref, acc_ref, *, nk):
  k = pl.program_id(2)

  @pl.when(k == 0)
  def _():
    acc_ref[...] = jnp.zeros_like(acc_ref)

  c = c_ref[...]
  c_act = (c * jax.nn.sigmoid(c)).astype(BF16)
  acc_ref[...] += jnp.dot(c_act, w_ref[...].astype(BF16), preferred_element_type=F32)

  @pl.when(k == nk - 1)
  def _():
    o_ref[...] = acc_ref[...] + b_ref[...]


def _ada_all(c_all, w_ada, b_ada):
  depth, d, n = w_ada.shape
  mp = c_all.shape[0]
  tk, tn = _tile(d, 1024), _tile(n, 2048)
  nk = d // tk
  blocks = (_nbytes((mp, tk), F32) + _nbytes((tk, tn), F32) + _nbytes((1, tn), F32)
            + _nbytes((mp, tn), F32))
  return pl.pallas_call(
      functools.partial(_ada_kernel, nk=nk),
      out_shape=jax.ShapeDtypeStruct((depth, mp, n), F32),
      grid=(depth, n // tn, nk),
      in_specs=[pl.BlockSpec((mp, tk), lambda l, j, k: (0, k)),
                pl.BlockSpec((None, tk, tn), lambda l, j, k: (l, k, j)),
                pl.BlockSpec((None, 1, tn), lambda l, j, k: (l, 0, j))],
      out_specs=pl.BlockSpec((None, mp, tn), lambda l, j, k: (l, 0, j)),
      scratch_shapes=[pltpu.VMEM((mp, tn), F32)],
      compiler_params=pltpu.CompilerParams(
          dimension_semantics=("parallel", "parallel", "arbitrary"),
          vmem_limit_bytes=_vmem_limit(blocks, _nbytes((mp, tn), F32),
                                       _nbytes((tk, tn), BF16))),
      name="ada_mod",
  )(c_all, w_ada, b_ada.reshape(depth, 1, n))


class _Pass:
  def __init__(self, name, b, t, d, time_major):
    self.name, self.b, self.t, self.d, self.time_major = name, b, t, d, time_major
    self.m = b * t
    self.tm = b if time_major else _tile(t, CHUNK, V7X_SUBLANES)
    self.n_tiles = self.m // self.tm
    self.tiles_per_seq = 1 if time_major else t // self.tm

  def row_spec(self, width, col=0):
    return pl.BlockSpec((self.tm, width), lambda i, c=col: (i, c))

  def mod_spec(self, which):
    d = self.d
    if self.time_major:
      return pl.BlockSpec((self.tm, d), lambda i, c=which: (0, c))
    tps = self.tiles_per_seq
    return pl.BlockSpec((None, 1, d), lambda i, c=which: (i // tps, 0, c))

  def mod_rows(self):
    return self.tm if self.time_major else 1


def _vec_spec(width):
  return pl.BlockSpec((1, width), lambda i: (0, 0))


def _full_spec(shape):
  nd = len(shape)
  return pl.BlockSpec(shape, lambda i, nd=nd: (0,) * nd)


def _row_params(blocks, temps=0):
  return pltpu.CompilerParams(
      dimension_semantics=("parallel",),
      vmem_limit_bytes=_vmem_limit(blocks, 0, temps))


def _rms(x, g):
  ms = jnp.mean(x * x, axis=-1, keepdims=True)
  return x * lax.rsqrt(ms + EPS_RMS) * g


def _prenorm_kernel(x_ref, g_ref, sc_ref, sh_ref, h_ref):
  h = _rms(x_ref[...], g_ref[...]) * (1.0 + sc_ref[...]) + sh_ref[...]
  h_ref[...] = h.astype(h_ref.dtype)


def _resid_kernel(x_ref, y_ref, gt_ref, gpost_ref, xo_ref):
  xo_ref[...] = x_ref[...] + gt_ref[...] * _rms(y_ref[...], gpost_ref[...])


def _resid_prenorm_kernel(x_ref, y_ref, gt_ref, gpost_ref, gpre_ref, sc_ref, sh_ref,
                          xo_ref, h_ref):
  xo = x_ref[...] + gt_ref[...] * _rms(y_ref[...], gpost_ref[...])
  xo_ref[...] = xo
  h = _rms(xo, gpre_ref[...]) * (1.0 + sc_ref[...]) + sh_ref[...]
  h_ref[...] = h.astype(h_ref.dtype)


def _prenorm(p, x, mod, g_pre, sc_idx, sh_idx):
  d = p.d
  blocks = _nbytes((p.tm, d), F32) + _nbytes((p.tm, d), BF16) + 3 * _nbytes((p.mod_rows(), d), F32)
  return pl.pallas_call(
      _prenorm_kernel,
      out_shape=jax.ShapeDtypeStruct((p.m, d), BF16),
      grid=(p.n_tiles,),
      in_specs=[p.row_spec(d), _vec_spec(d), p.mod_spec(sc_idx), p.mod_spec(sh_idx)],
      out_specs=p.row_spec(d),
      compiler_params=_row_params(blocks, 3 * _nbytes((p.tm, d), F32)),
      name=f"prenorm_{p.name}",
  )(x, g_pre.reshape(1, d), mod, mod)


def _resid(p, x, y, mod, gt_idx, g_post):
  d = p.d
  blocks = 3 * _nbytes((p.tm, d), F32) + 2 * _nbytes((p.mod_rows(), d), F32)
  return pl.pallas_call(
      _resid_kernel,
      out_shape=jax.ShapeDtypeStruct((p.m, d), F32),
      grid=(p.n_tiles,),
      in_specs=[p.row_spec(d), p.row_spec(d), p.mod_spec(gt_idx), _vec_spec(d)],
      out_specs=p.row_spec(d),
      compiler_params=_row_params(blocks, 3 * _nbytes((p.tm, d), F32)),
      name=f"resid_{p.name}",
  )(x, y, mod, g_post.reshape(1, d))


def _resid_prenorm(p, x, y, mod, gt_idx, g_post, mod_pre, g_pre, sc_idx, sh_idx):
  d = p.d
  blocks = (3 * _nbytes((p.tm, d), F32) + _nbytes((p.tm, d), BF16)
            + 5 * _nbytes((p.mod_rows(), d), F32))
  return pl.pallas_call(
      _resid_prenorm_kernel,
      out_shape=(jax.ShapeDtypeStruct((p.m, d), F32), jax.ShapeDtypeStruct((p.m, d), BF16)),
      grid=(p.n_tiles,),
      in_specs=[p.row_spec(d), p.row_spec(d), p.mod_spec(gt_idx), _vec_spec(d), _vec_spec(d),
                p.mod_spec(sc_idx), p.mod_spec(sh_idx)],
      out_specs=(p.row_spec(d), p.row_spec(d)),
      compiler_params=_row_params(blocks, 4 * _nbytes((p.tm, d), F32)),
      name=f"resid_prenorm_{p.name}",
  )(x, y, mod, g_post.reshape(1, d), g_pre.reshape(1, d), mod_pre, mod_pre)


def _gelu_exact(x):
  return 0.5 * x * (1.0 + lax.erf(x * (0.5 ** 0.5)))


def _layer_norm(x, w, b):
  mu = jnp.mean(x, axis=-1, keepdims=True)
  xc = x - mu
  var = jnp.mean(xc * xc, axis=-1, keepdims=True)
  return xc * lax.rsqrt(var + EPS_LN) * w + b


def _sgu_prompt_kernel(z_ref, lnw_ref, lnb_ref, ws_ref, bs_ref, oa_ref, *, d_a):
  dh = d_a // H_A
  ga = _gelu_exact(z_ref[...])
  u = ga[:, :d_a]
  van = _layer_norm(ga[:, d_a:], lnw_ref[...], lnb_ref[...]).astype(BF16)
  row = lax.broadcasted_iota(jnp.int32, (CHUNK, CHUNK), 0)
  col = lax.broadcasted_iota(jnp.int32, (CHUNK, CHUNK), 1)
  causal = row >= col
  for h in range(H_A):
    sl = slice(h * dh, (h + 1) * dh)
    w_h = jnp.where(causal, ws_ref[h], 0.0).astype(BF16)
    s_h = jnp.dot(w_h, van[:, sl], preferred_element_type=F32) + bs_ref[:, h:h + 1]
    oa_ref[:, sl] = (u[:, sl] * s_h).astype(oa_ref.dtype)


def _sgu_prompt(p, z, ln_w, ln_b, w_sp, b_sp, d_a):
  assert p.tm == CHUNK
  blocks = (_nbytes((CHUNK, 2 * d_a), F32) + _nbytes((H_A, CHUNK, CHUNK), F32)
            + _nbytes((CHUNK, d_a), BF16))
  return pl.pallas_call(
      functools.partial(_sgu_prompt_kernel, d_a=d_a),
      out_shape=jax.ShapeDtypeStruct((p.m, d_a), BF16),
      grid=(p.n_tiles,),
      in_specs=[p.row_spec(2 * d_a), _vec_spec(d_a), _vec_spec(d_a),
                _full_spec((H_A, CHUNK, CHUNK)), _full_spec((CHUNK, H_A))],
      out_specs=p.row_spec(d_a),
      compiler_params=_row_params(blocks, 4 * _nbytes((CHUNK, 2 * d_a), F32)),
      name=f"sgu_{p.name}",
  )(z, ln_w.reshape(1, d_a), ln_b.reshape(1, d_a), w_sp[:, :CHUNK, :CHUNK],
    b_sp[:, :CHUNK].T)


def _sgu_sample_kernel(z_ref, lnw_ref, lnb_ref, wsv_ref, bsv_ref, oa_ref, vn_ref, *, d_a, t, b):
  ga = _gelu_exact(z_ref[...])
  van = _layer_norm(ga[:, d_a:], lnw_ref[...], lnb_ref[...])
  vn_ref[...] = van
  for i in range(t):
    s = bsv_ref[i:i + 1, :]
    for j in range(i + 1):
      s = s + wsv_ref[i * t + j:i * t + j + 1, :] * van[j * b:(j + 1) * b, :]
    oa_ref[i * b:(i + 1) * b, :] = (ga[i * b:(i + 1) * b, :d_a] * s).astype(oa_ref.dtype)


def _sgu_sample(p, z, ln_w, ln_b, w_sp, b_sp, d_a):
  t, b = p.t, p.b
  dh = d_a // H_A
  wsv = jnp.repeat(w_sp[:, :t, :t].transpose(1, 2, 0).reshape(t * t, H_A), dh, axis=1)
  bsv = jnp.repeat(b_sp[:, :t].T, dh, axis=1)
  blocks = (_nbytes((p.m, 2 * d_a), F32) + _nbytes((p.m, d_a), BF16) + _nbytes((p.m, d_a), F32))
  return pl.pallas_call(
      functools.partial(_sgu_sample_kernel, d_a=d_a, t=t, b=b),
      out_shape=(jax.ShapeDtypeStruct((p.m, d_a), BF16), jax.ShapeDtypeStruct((p.m, d_a), F32)),
      grid=(1,),
      in_specs=[pl.BlockSpec((p.m, 2 * d_a), lambda i: (0, 0)), _vec_spec(d_a), _vec_spec(d_a),
                _full_spec((t * t, d_a)), _full_spec((t, d_a))],
      out_specs=(_full_spec((p.m, d_a)), _full_spec((p.m, d_a))),
      compiler_params=pltpu.CompilerParams(
          dimension_semantics=("arbitrary",),
          vmem_limit_bytes=_vmem_limit(blocks, 0, 3 * _nbytes((p.m, 2 * d_a), F32))),
      name=f"sgu_{p.name}",
  )(z, ln_w.reshape(1, d_a), ln_b.reshape(1, d_a), wsv, bsv)


def _rwkv_math(cur, delta, mu_ref, bias_ref, lora_refs, v_first):
  rc, kc, vc, xc = cur
  rd, kd, vd, xd = delta
  w1_ref, w2_ref, a1_ref, a2_ref, g1_ref, g2_ref, v1_ref, v2_ref = lora_refs

  def lora(x, m1_ref, m2_ref, mid=None):
    hcol = jnp.dot(x.astype(BF16), m1_ref[...], preferred_element_type=F32)
    if mid is not None:
      hcol = mid(hcol)
    return jnp.dot(hcol.astype(BF16), m2_ref[...], preferred_element_type=F32)

  mu = lambda i: mu_ref[i:i + 1, :]
  bias = lambda i: bias_ref[i:i + 1, :]
  lerp = lambda c, dl, m: c + dl * m

  r = lerp(rc, rd, mu(0))
  k = lerp(kc, kd, mu(1))
  v = lerp(vc, vd, mu(2))
  u = bias(0) + lora(lerp(xc, xd, mu(3)), w1_ref, w2_ref, jnp.tanh)
  w = jnp.exp(-jnp.exp(-jax.nn.softplus(-u) - 0.5))
  if v_first is not None:
    gate = jax.nn.sigmoid(bias(2) + lora(lerp(xc, xd, mu(6)), v1_ref, v2_ref))
    v = v + (v_first - v) * gate
  ag = jax.nn.sigmoid(bias(1) + lora(lerp(xc, xd, mu(4)), a1_ref, a2_ref))
  g = lora(lerp(xc, xd, mu(5)), g1_ref, g2_ref, jax.nn.sigmoid)
  return r, w, k, v, ag, g


def _rwkv_prep_sample_kernel(*refs, has_vfirst):
  cur_refs, prev_refs = refs[0:4], refs[4:8]
  mu_ref, bias_ref = refs[8:10]
  lora_refs = refs[10:18]
  pos = 18
  vf = None
  if has_vfirst:
    vf = refs[pos][...]
    pos += 1
  out_refs = refs[pos:pos + 6]
  cur = [c[...] for c in cur_refs]
  delta = [p[...] - c for p, c in zip(prev_refs, cur)]
  outs = _rwkv_math(cur, delta, mu_ref, bias_ref, lora_refs, vf)
  for o_ref, val in zip(out_refs, outs):
    o_ref[...] = val


def _rwkv_prep_sample(p, z, prev, mu7, bias3, loras, v_first, d_b, col0):
  has_vfirst = v_first is not None
  ins = [z] * 4 + [prev] * 4 + [mu7, bias3] + list(loras)
  specs = [p.row_spec(d_b, col0 + s) for s in range(4)] + [p.row_spec(d_b, s) for s in range(4)]
  specs += [_full_spec(mu7.shape), _full_spec(bias3.shape)] + [_full_spec(m.shape) for m in loras]
  if has_vfirst:
    ins.append(v_first)
    specs.append(p.row_spec(d_b))
  tile = _nbytes((p.tm, d_b), F32)
  out = jax.ShapeDtypeStruct((p.m, d_b), F32)
  return pl.pallas_call(
      functools.partial(_rwkv_prep_sample_kernel, has_vfirst=has_vfirst),
      out_shape=(out,) * 6,
      grid=(p.n_tiles,),
      in_specs=specs,
      out_specs=(p.row_spec(d_b),) * 6,
      compiler_params=_row_params(15 * tile + 8 * _nbytes(loras[0].shape, BF16), 10 * tile),
      name=f"rwkv_prep_{p.name}",
  )(*ins)


def _interleave(vals, h_b):
  nb = len(vals)
  width = nb * h_b
  lane = lax.broadcasted_iota(jnp.int32, (vals[0].shape[0], width), 1)
  tiles = []
  for j in range(HEAD_B):
    c, q = divmod(j, nb)
    tile = None
    for b in range(nb):
      src = vals[b][:, c * width:(c + 1) * width]
      if b != q:
        src = pltpu.roll(src, ((b - q) * h_b) % width, axis=1)
      tile = src if tile is None else jnp.where(lane >= b * h_b, src, tile)
    tiles.append(tile)
  return jnp.concatenate(tiles, axis=1)


def _deinterleave(row_ref, b, nb, h_b):
  width = nb * h_b
  lane = lax.broadcasted_iota(jnp.int32, (row_ref.shape[0], width), 1)
  cols = []
  for c in range(HEAD_B // nb):
    col = None
    for q in range(nb):
      src = row_ref[:, (nb * c + q) * width:(nb * c + q + 1) * width]
      if q != b:
        src = pltpu.roll(src, ((q - b) * h_b) % width, axis=1)
      col = src if col is None else jnp.where(lane >= q * h_b, src, col)
    cols.append(col)
  return jnp.concatenate(cols, axis=1)


def _rwkv_prep_prompt_kernel(*refs, nb, tm, h_b, has_vfirst, emit_v):
  cur_refs, prev_refs = refs[0:4], refs[4:8]
  mu_ref, bias_ref = refs[8:10]
  lora_refs = refs[10:18]
  pos = 18
  vf = None
  rows = nb * tm
  if has_vfirst:
    vf = refs[pos][...].reshape(rows, -1)
    pos += 1
  row_out_refs = refs[pos:pos + 5]
  g_ref = refs[pos + 5]
  pos += 6

  cur, delta = [], []
  for c_ref, p_ref in zip(cur_refs, prev_refs):
    c = c_ref[...].reshape(rows, -1)
    prev = pltpu.roll(c, 1, axis=0)
    row = lax.broadcasted_iota(jnp.int32, c.shape, 0)
    for b in range(nb):
      prev = jnp.where(row == b * tm, p_ref[b:b + 1, :], prev)
    cur.append(c)
    delta.append(prev - c)
  r, w, k, v, ag, g = _rwkv_math(cur, delta, mu_ref, bias_ref, lora_refs, vf)
  for o_ref, val in zip(row_out_refs, (r, w, k, v, ag)):
    o_ref[...] = _interleave([val[b * tm:(b + 1) * tm] for b in range(nb)], h_b)
  g_ref[...] = g.reshape(nb, tm, -1)
  if emit_v:
    refs[pos][...] = v.reshape(nb, tm, -1)


def _rwkv_prep_prompt(p, z, prev, mu7, bias3, loras, v_first, d_b, col0, emit_v):
  nb, t = p.b, p.t
  h_b = d_b // HEAD_B
  tm = _tile(t, 32, V7X_SUBLANES)
  has_vfirst = v_first is not None
  tok_spec = lambda c=0: pl.BlockSpec((nb, tm, d_b), lambda i, c=c: (0, i, c))
  ins = [z] * 4 + [prev] * 4 + [mu7, bias3] + list(loras)
  specs = [tok_spec(col0 + s) for s in range(4)]
  specs += [pl.BlockSpec((None, nb, d_b), lambda i, c=s: (i, 0, c)) for s in range(4)]
  specs += [_full_spec(mu7.shape), _full_spec(bias3.shape)] + [_full_spec(m.shape) for m in loras]
  if has_vfirst:
    ins.append(v_first)
    specs.append(tok_spec())
  width = HEAD_B * nb * h_b
  row_shape = jax.ShapeDtypeStruct((t, width), F32)
  tok_shape = jax.ShapeDtypeStruct((nb, t, d_b), F32)
  out_shape = (row_shape,) * 5 + (tok_shape,) + ((tok_shape,) if emit_v else ())
  out_specs = ((pl.BlockSpec((tm, width), lambda i: (i, 0)),) * 5 + (tok_spec(),)
               + ((tok_spec(),) if emit_v else ()))
  tile = _nbytes((nb * tm, d_b), F32)
  return pl.pallas_call(
      functools.partial(_rwkv_prep_prompt_kernel, nb=nb, tm=tm, h_b=h_b, has_vfirst=has_vfirst,
                        emit_v=emit_v),
      out_shape=out_shape,
      grid=(t // tm,),
      in_specs=specs,
      out_specs=out_specs,
      compiler_params=_row_params(12 * tile + 8 * _nbytes(loras[0].shape, BF16), 16 * tile),
      name=f"rwkv_prep_{p.name}",
  )(*ins)


def _wkv_kernel(r_ref, w_ref, k_ref, v_ref, ag_ref, kk_ref, ka_ref, rk_ref, lnw_ref, lnb_ref,
                s0_ref, o_ref, s_ref, a_scr, b_scr, km_scr, yr_scr, vt_scr, yt_scr, *, tb, n, l):
  @pl.when(pl.program_id(1) == 0)
  def _():
    s_ref[...] = s0_ref[...]

  def tile_sum(x):
    acc = x[:, 0:l]
    for j in range(1, n):
      acc = acc + x[:, j * l:(j + 1) * l]
    return acc

  rep = lambda x: jnp.concatenate([x] * n, axis=1)
  row = lambda ref, u, j: ref[u:u + 1, j * l:(j + 1) * l]

  k_raw = k_ref[...]
  ag = ag_ref[...]
  kk = k_raw * kk_ref[...]
  kk = kk / rep(jnp.maximum(jnp.sqrt(tile_sum(kk * kk)), 1e-12))
  a_scr[0:tb, :] = -kk
  a_scr[tb:tb + V7X_SUBLANES, :] = jnp.zeros((V7X_SUBLANES, n * l), F32)
  b_scr[...] = kk * ag
  km_scr[...] = k_raw * (1.0 + (ag - 1.0) * ka_ref[...])

  sa0 = jnp.zeros((n, l), F32)
  for j in range(n):
    sa0 = sa0 + s_ref[j] * row(a_scr, 0, j)

  group = V7X_SUBLANES if tb % V7X_SUBLANES == 0 else tb

  def step_group(gi, sa):
    if tb == group:
      win = lambda ref, off=0: ref
      a_next = None
    else:
      base = pl.multiple_of(gi * group, group)
      win = lambda ref, off=0: ref.at[pl.ds(pl.multiple_of(base + off, group), group), :]
      a_next = win(a_scr, group)
    w_w, b_w, k_w, r_w, v_w, a_w, y_w = (win(w_ref), win(b_scr), win(km_scr), win(r_ref),
                                         win(v_ref), win(a_scr), win(yr_scr))
    for u in range(group):
      for i in range(n):
        vt_scr[i:i + 1, :] = row(v_w, u, i)
      v_t = vt_scr[...]
      y = jnp.zeros_like(sa)
      sa_next = jnp.zeros_like(sa)
      for j in range(n):
        s_new = s_ref[j] * row(w_w, u, j) + sa * row(b_w, u, j) + v_t * row(k_w, u, j)
        s_ref[j] = s_new
        y = y + s_new * row(r_w, u, j)
        if u + 1 < group or a_next is None:
          sa_next = sa_next + s_new * row(a_w, u + 1, j)
        else:
          sa_next = sa_next + s_new * row(a_next, 0, j)
      yt_scr[...] = y
      for i in range(n):
        y_w[u:u + 1, i * l:(i + 1) * l] = yt_scr[i:i + 1, :]
      sa = sa_next
    return sa

  if tb == group:
    step_group(0, sa0)
  else:
    lax.fori_loop(0, tb // group, step_group, sa0)

  y = yr_scr[...]
  yc = y - rep(tile_sum(y) * (1.0 / n))
  var = tile_sum(yc * yc) * (1.0 / n)
  yn = yc * rep(lax.rsqrt(var + EPS_GN)) * lnw_ref[...] + lnb_ref[...]
  rk = tile_sum(r_ref[...] * km_scr[...] * rk_ref[...])
  o_ref[...] = yn + rep(rk) * v_ref[...]


def _wkv(seq, params, s0, *, tb, name):
  g, t, width = seq[0].shape
  n = HEAD_B
  l = width // n
  tb = _tile(t, tb, V7X_SUBLANES)
  seq_spec = pl.BlockSpec((None, tb, width), lambda gi, ti: (gi, ti, 0))
  par_spec = pl.BlockSpec((None, 1, width), lambda gi, ti: (gi, 0, 0))
  st_spec = pl.BlockSpec((None, n, n, l), lambda gi, ti: (gi, 0, 0, 0))
  blk = _nbytes((tb, width), F32)
  st = _nbytes((n, n, l), F32)
  return pl.pallas_call(
      functools.partial(_wkv_kernel, tb=tb, n=n, l=l),
      out_shape=(jax.ShapeDtypeStruct((g, t, width), F32), jax.ShapeDtypeStruct((g, n, n, l), F32)),
      grid=(g, t // tb),
      in_specs=[seq_spec] * 5 + [par_spec] * 5 + [st_spec],
      out_specs=(seq_spec, st_spec),
      scratch_shapes=[pltpu.VMEM((tb + V7X_SUBLANES, width), F32), pltpu.VMEM((tb, width), F32),
                      pltpu.VMEM((tb, width), F32), pltpu.VMEM((tb, width), F32),
                      pltpu.VMEM((n, l), F32), pltpu.VMEM((n, l), F32)],
      compiler_params=pltpu.CompilerParams(
          dimension_semantics=("parallel", "arbitrary"),
          vmem_limit_bytes=_vmem_limit(6 * blk + 2 * st + 5 * _nbytes((1, width), F32),
                                       6 * blk, 10 * blk)),
      name=name,
  )(*seq, *params, s0)


def _cat_sample_kernel(oa_ref, ob_ref, g_ref, o_ref, *, d_a):
  o_ref[:, :d_a] = oa_ref[...]
  o_ref[:, d_a:] = (ob_ref[...] * g_ref[...]).astype(o_ref.dtype)


def _cat_sample(p, oa, ob, g, d_a, d_b):
  d = d_a + d_b
  blocks = _nbytes((p.tm, d_a), BF16) + 2 * _nbytes((p.tm, d_b), F32) + _nbytes((p.tm, d), BF16)
  return pl.pallas_call(
      functools.partial(_cat_sample_kernel, d_a=d_a),
      out_shape=jax.ShapeDtypeStruct((p.m, d), BF16),
      grid=(p.n_tiles,),
      in_specs=[p.row_spec(d_a), p.row_spec(d_b), p.row_spec(d_b)],
      out_specs=p.row_spec(d),
      compiler_params=_row_params(blocks, _nbytes((p.tm, d_b), F32)),
      name=f"cat_{p.name}",
  )(oa, ob, g)


def _cat_prompt_kernel(oa_ref, ob_ref, g_ref, o_ref, *, nb, h_b, d_a):
  o_ref[:, :, :d_a] = oa_ref[...]
  for b in range(nb):
    ob = _deinterleave(ob_ref, b, nb, h_b)
    o_ref[b, :, d_a:] = (ob * g_ref[b]).astype(o_ref.dtype)


def _cat_prompt(p, oa, ob, g, d_a, d_b):
  nb, t = p.b, p.t
  h_b = d_b // HEAD_B
  d = d_a + d_b
  tm = _tile(t, 64, V7X_SUBLANES)
  width = ob.shape[1]
  blocks = (_nbytes((nb, tm, d_a), BF16) + _nbytes((tm, width), F32) + _nbytes((nb, tm, d_b), F32)
            + _nbytes((nb, tm, d), BF16))
  return pl.pallas_call(
      functools.partial(_cat_prompt_kernel, nb=nb, h_b=h_b, d_a=d_a),
      out_shape=jax.ShapeDtypeStruct((nb, t, d), BF16),
      grid=(t // tm,),
      in_specs=[pl.BlockSpec((nb, tm, d_a), lambda i: (0, i, 0)),
                pl.BlockSpec((tm, width), lambda i: (i, 0)),
                pl.BlockSpec((nb, tm, d_b), lambda i: (0, i, 0))],
      out_specs=pl.BlockSpec((nb, tm, d), lambda i: (0, i, 0)),
      compiler_params=_row_params(blocks, 3 * _nbytes((tm, width), F32)),
      name=f"cat_{p.name}",
  )(oa, ob, g)


def _to_jh(x, h_b):
  s = x.shape[:-1]
  return x.reshape(s + (h_b, HEAD_B)).swapaxes(-1, -2).reshape(s + (h_b * HEAD_B,))


def _from_jh(x, h_b):
  s = x.shape[:-1]
  return x.reshape(s + (HEAD_B, h_b)).swapaxes(-1, -2).reshape(s + (h_b * HEAD_B,))


def _slabs_to_jh(x, h_b):
  s = x.shape[:-1]
  return _to_jh(x.reshape(s + (4, h_b * HEAD_B)), h_b).reshape(s + (4 * h_b * HEAD_B,))


def _slabs_from_jh(x, h_b):
  s = x.shape[:-1]
  return _from_jh(x.reshape(s + (4, h_b * HEAD_B)), h_b).reshape(s + (4 * h_b * HEAD_B,))


def _scan_param(p, v, h_b):
  vh = v.reshape(h_b, HEAD_B)
  if p.time_major:
    return jnp.repeat(vh, p.b, axis=1).reshape(h_b, 1, HEAD_B * p.b)
  return jnp.tile(vh.T, (1, p.b)).reshape(1, 1, HEAD_B * p.b * h_b)


def _state_to_scan(p, s):
  if p.time_major:
    return s.transpose(1, 3, 2, 0)
  b, h_b = s.shape[:2]
  return s.transpose(3, 2, 0, 1).reshape(1, HEAD_B, HEAD_B, b * h_b)


def _state_from_scan(p, s, h_b):
  if p.time_major:
    return s.transpose(3, 0, 2, 1)
  return s.reshape(HEAD_B, HEAD_B, p.b, h_b).transpose(2, 3, 1, 0)


def _trunk(p, x, mod_l, shift0, wkv0, weights, keep_chunk_rows):
  (g_pre_mix, g_post_mix, g_pre_ffn, g_post_ffn, w_in, w_out, ln_v_w, ln_v_b, w_spatial,
   b_spatial, mu7s, bias3s, loras, k_k, k_a, r_k, ln_x_w, ln_x_b, w_up, w_down) = weights
  depth = w_in.shape[0]
  d = p.d
  d_a = d // 2
  d_b = d - d_a
  h_b = d_b // HEAD_B
  p_a = 2 * d_a
  p_in = p_a + 4 * d_b
  assert p_a % d_b == 0
  col0 = p_a // d_b
  mm_tm = 1024 if not p.time_major else 512
  new_wkv, new_shift, chunk_rows = [], [], []
  v_first = None
  h = _prenorm(p, x, mod_l[0], g_pre_mix[0], 1, 0)
  for l in range(depth):
    mod = mod_l[l]
    z = _matmul(h, w_in, l, F32, tm=mm_tm, name=f"mm_in_{p.name}")
    par = [_scan_param(p, t, h_b) for t in (k_k[l], k_a[l], r_k[l].reshape(-1), ln_x_w[l], ln_x_b[l])]
    s0 = _state_to_scan(p, wkv0[l])
    if p.time_major:
      out_a, va_n = _sgu_sample(p, z, ln_v_w[l], ln_v_b[l], w_spatial[l], b_spatial[l], d_a)
      if keep_chunk_rows:
        chunk_rows.append(va_n)
      zb = z[:, p_a:]
      prev = jnp.concatenate([shift0[l], zb[:-p.b]], axis=0)
      new_shift.append(zb[(p.t - 1) * p.b:])
      r, w, k, v, ag, g = _rwkv_prep_sample(p, z, prev, mu7s[l], bias3s[l], loras[l], v_first,
                                            d_b, col0)
      if l == 0:
        v_first = v
      to_scan = lambda a: (a.reshape(p.t, p.b, HEAD_B, h_b).transpose(3, 0, 2, 1)
                           .reshape(h_b, p.t, HEAD_B * p.b))
      ob, s_new = _wkv([to_scan(a) for a in (r, w, k, v, ag)], par, s0, tb=32,
                       name=f"wkv_{p.name}")
      ob = ob.reshape(h_b, p.t, HEAD_B, p.b).transpose(1, 3, 2, 0).reshape(p.m, d_b)
      cat = _cat_sample(p, out_a, ob, g, d_a, d_b)
    else:
      out_a = _sgu_prompt(p, z, ln_v_w[l], ln_v_b[l], w_spatial[l], b_spatial[l], d_a)
      z3 = z.reshape(p.b, p.t, p_in)
      tm_prep = _tile(p.t, 32, V7X_SUBLANES)
      zl = z3[:, tm_prep - 1::tm_prep, p_a:]
      prev = jnp.concatenate([shift0[l][:, None, :], zl[:, :-1]], axis=1).transpose(1, 0, 2)
      new_shift.append(zl[:, -1])
      vf3 = None if v_first is None else v_first
      outs = _rwkv_prep_prompt(p, z3, prev, mu7s[l], bias3s[l], loras[l], vf3, d_b, col0,
                               emit_v=(l == 0))
      if l == 0:
        v_first = outs[6]
      seq = [a[None] for a in outs[:5]]
      ob, s_new = _wkv(seq, par, s0, tb=32, name=f"wkv_{p.name}")
      cat = _cat_prompt(p, out_a.reshape(p.b, p.t, d_a), ob[0], outs[5], d_a, d_b)
      cat = cat.reshape(p.m, d)
    new_wkv.append(_state_from_scan(p, s_new, h_b))
    y = _matmul(cat, w_out, l, F32, tm=mm_tm, name=f"mm_out_{p.name}")
    x, h = _resid_prenorm(p, x, y, mod, 2, g_post_mix[l], mod, g_pre_ffn[l], 4, 3)
    f = _matmul(h, w_up, l, BF16, act=_relu2, tm=mm_tm, name=f"mm_up_{p.name}")
    f = _matmul(f, w_down, l, F32, tm=mm_tm, name=f"mm_down_{p.name}")
    if l + 1 < depth:
      x, h = _resid_prenorm(p, x, f, mod, 5, g_post_ffn[l], mod_l[l + 1], g_pre_mix[l + 1], 1, 0)
    else:
      x = _resid(p, x, f, mod, 5, g_post_ffn[l])
  rows = jnp.stack(chunk_rows) if keep_chunk_rows else None
  return x, jnp.stack(new_wkv), jnp.stack(new_shift), rows


def _pad_lora(m1, m2, h_b):
  rank = m1.shape[1]
  rp = -(-rank // V7X_LANES) * V7X_LANES
  m1 = jnp.pad(_to_jh(m1.T, h_b).T, ((0, 0), (0, rp - rank))).astype(BF16)
  m2 = jnp.pad(_to_jh(m2, h_b), ((0, rp - rank), (0, 0))).astype(BF16)
  return m1, m2


def kernel(x_prompt, x_sample, state_wkv, state_shift, c_prompt, c_sample, w_ada, b_ada, g_pre_mix, g_post_mix, g_pre_ffn, g_post_ffn, w_in, w_out, ln_v_w, ln_v_b, w_spatial, b_spatial, mu_rkv, mu_lora, mu_vm, w0, w1, w2, a0, a1, a2, v0, v1, v2, g1, g2, k_k, k_a, r_k, ln_x_w, ln_x_b, w_up, w_down):
  bp, tp, d = x_prompt.shape
  bs, ts, _ = x_sample.shape
  depth = w_in.shape[0]
  d_a = d // 2
  d_b = d - d_a
  h_b = d_b // HEAD_B
  p_a = 2 * d_a

  n_c = bs + bp
  c_all = jnp.concatenate([c_sample, c_prompt, jnp.zeros((-n_c % V7X_SUBLANES, d), F32)], axis=0)
  mod_all = _ada_all(c_all, w_ada, b_ada)
  mod_s = [mod_all[l, :bs] for l in range(depth)]
  mod_p = [mod_all[l, bs:n_c].reshape(bp, 1, N_MOD * d) for l in range(depth)]

  w_in_p = jnp.concatenate([w_in[..., :p_a], _slabs_to_jh(w_in[..., p_a:], h_b)],
                           axis=-1).astype(BF16)
  w_out_b = w_out[:, d_a:].reshape(depth, h_b, HEAD_B, d).swapaxes(1, 2).reshape(depth, d_b, d)
  w_out_p = jnp.concatenate([w_out[:, :d_a], w_out_b], axis=1).astype(BF16)
  loras, mu7s, bias3s = [], [], []
  for l in range(depth):
    lv = max(l - 1, 0)
    loras.append(_pad_lora(w1[l], w2[l], h_b) + _pad_lora(a1[l], a2[l], h_b)
                 + _pad_lora(g1[l], g2[l], h_b) + _pad_lora(v1[lv], v2[lv], h_b))
    mu7s.append(_to_jh(jnp.concatenate([mu_rkv[l], mu_lora[l], mu_vm[lv][None]], axis=0), h_b))
    bias3s.append(_to_jh(jnp.stack([w0[l], a0[l], v0[lv]]), h_b))
  weights = (g_pre_mix, g_post_mix, g_pre_ffn, g_post_ffn, w_in_p, w_out_p, ln_v_w, ln_v_b,
             w_spatial, b_spatial, mu7s, bias3s, loras, k_k, k_a, r_k, ln_x_w, ln_x_b,
             w_up.astype(BF16), w_down.astype(BF16))

  pp = _Pass("prompt", bp, tp, d, time_major=False)
  shift0 = jnp.zeros((depth, bp, 4 * d_b), x_prompt.dtype)
  wkv0 = jnp.zeros((depth, bp, h_b, HEAD_B, HEAD_B), state_wkv.dtype)
  y_p, wkv_p, shift_p, _ = _trunk(pp, x_prompt.reshape(bp * tp, d), mod_p, shift0, wkv0,
                                  weights, False)

  ps = _Pass("sample", bs, ts, d, time_major=True)
  x_s = x_sample.transpose(1, 0, 2).reshape(ts * bs, d)
  y_s, wkv_s, shift_s, rows_s = _trunk(ps, x_s, mod_s, _slabs_to_jh(state_shift, h_b), state_wkv,
                                       weights, True)

  y_sample = y_s.reshape(ts, bs, d).transpose(1, 0, 2)
  chunk_v = rows_s.reshape(depth, ts, bs, d_a).transpose(0, 2, 1, 3)
  return (y_p.reshape(bp, tp, d), y_sample, wkv_p, _slabs_from_jh(shift_p, h_b), wkv_s,
          _slabs_from_jh(shift_s, h_b), chunk_v)
```

```python
import functools

import jax
import jax.numpy as jnp
from jax import lax
from jax.experimental import pallas as pl
from jax.experimental.pallas import tpu as pltpu

F32 = jnp.float32
BF16 = jnp.bfloat16

H_A = 8
CHUNK = 128
HEAD_B = 64
N_MOD = 6
EPS_RMS = 1e-6
EPS_LN = 1e-5
EPS_GN = 64e-5

V7X_LANES = 128
V7X_SUBLANES = 8
VMEM_LIMIT_CAP = 56 * 1024 * 1024


def _vmem_limit(block_bytes, scratch_bytes=0, temp_bytes=0):
  est = 2 * block_bytes + scratch_bytes + temp_bytes + (2 << 20)
  return int(min(max(est, 16 << 20), VMEM_LIMIT_CAP))


def _tile(dim, target, align=V7X_LANES):
  if dim <= target:
    return dim
  t = (target // align) * align
  while t >= align:
    if dim % t == 0:
      return t
    t -= align
  return dim


def _nbytes(shape, dtype):
  n = 1
  for s in shape:
    n *= s
  return n * jnp.dtype(dtype).itemsize


def _relu2(x):
  r = jnp.maximum(x, 0.0)
  return r * r


def _mm_kernel(a_ref, b_ref, o_ref, *scratch, nk, act):
  if nk == 1:
    acc = jnp.dot(a_ref[...], b_ref[...], preferred_element_type=F32)
    o_ref[...] = (act(acc) if act else acc).astype(o_ref.dtype)
    return
  (acc_ref,) = scratch
  k = pl.program_id(2)

  @pl.when(k == 0)
  def _():
    acc_ref[...] = jnp.zeros_like(acc_ref)

  acc_ref[...] += jnp.dot(a_ref[...], b_ref[...], preferred_element_type=F32)

  @pl.when(k == nk - 1)
  def _():
    acc = acc_ref[...]
    o_ref[...] = (act(acc) if act else acc).astype(o_ref.dtype)


def _matmul(a, w, layer, out_dtype, *, act=None, tm=1024, tn=1024, tk=4096, name="matmul"):
  m, kdim = a.shape
  n = w.shape[2]
  tm, tn, tk = _tile(m, tm, V7X_SUBLANES), _tile(n, tn), _tile(kdim, tk)
  nk = kdim // tk
  blocks = (_nbytes((tm, tk), a.dtype) + _nbytes((tk, tn), w.dtype)
            + _nbytes((tm, tn), out_dtype))
  acc_bytes = _nbytes((tm, tn), F32)
  return pl.pallas_call(
      functools.partial(_mm_kernel, nk=nk, act=act),
      out_shape=jax.ShapeDtypeStruct((m, n), out_dtype),
      grid=(m // tm, n // tn, nk),
      in_specs=[pl.BlockSpec((tm, tk), lambda i, j, k: (i, k)),
                pl.BlockSpec((None, tk, tn), lambda i, j, k: (layer, k, j))],
      out_specs=pl.BlockSpec((tm, tn), lambda i, j, k: (i, j)),
      scratch_shapes=[pltpu.VMEM((tm, tn), F32)] if nk > 1 else [],
      compiler_params=pltpu.CompilerParams(
          dimension_semantics=("parallel", "parallel", "arbitrary"),
          vmem_limit_bytes=_vmem_limit(blocks, acc_bytes, 2 * acc_bytes)),
      name=name,
  )(a, w)


def _ada_kernel(c_ref, w_ref, b_ref, o_ref, acc_ref, *, nk):
  k = pl.program_id(2)

  @pl.when(k == 0)
  def _():
    acc_ref[...] = jnp.zeros_like(acc_ref)

  c = c_ref[...]
  c_act = (c * jax.nn.sigmoid(c)).astype(BF16)
  acc_ref[...] += jnp.dot(c_act, w_ref[...].astype(BF16), preferred_element_type=F32)

  @pl.when(k == nk - 1)
  def _():
    o_ref[...] = acc_ref[...] + b_ref[...]


def _ada_all(c_all, w_ada, b_ada):
  depth, d, n = w_ada.shape
  mp = c_all.shape[0]
  tk, tn = _tile(d, 1024), _tile(n, 2048)
  nk = d // tk
  blocks = (_nbytes((mp, tk), F32) + _nbytes((tk, tn), F32) + _nbytes((1, tn), F32)
            + _nbytes((mp, tn), F32))
  return pl.pallas_call(
      functools.partial(_ada_kernel, nk=nk),
      out_shape=jax.ShapeDtypeStruct((depth, mp, n), F32),
      grid=(depth, n // tn, nk),
      in_specs=[pl.BlockSpec((mp, tk), lambda l, j, k: (0, k)),
                pl.BlockSpec((None, tk, tn), lambda l, j, k: (l, k, j)),
                pl.BlockSpec((None, 1, tn), lambda l, j, k: (l, 0, j))],
      out_specs=pl.BlockSpec((None, mp, tn), lambda l, j, k: (l, 0, j)),
      scratch_shapes=[pltpu.VMEM((mp, tn), F32)],
      compiler_params=pltpu.CompilerParams(
          dimension_semantics=("parallel", "parallel", "arbitrary"),
          vmem_limit_bytes=_vmem_limit(blocks, _nbytes((mp, tn), F32),
                                       _nbytes((tk, tn), BF16))),
      name="ada_mod",
  )(c_all, w_ada, b_ada.reshape(depth, 1, n))


class _Pass:
  def __init__(self, name, b, t, d, time_major):
    self.name, self.b, self.t, self.d, self.time_major = name, b, t, d, time_major
    self.m = b * t
    self.tm = b if time_major else _tile(t, CHUNK, V7X_SUBLANES)
    self.n_tiles = self.m // self.tm
    self.tiles_per_seq = 1 if time_major else t // self.tm

  def row_spec(self, width, col=0):
    return pl.BlockSpec((self.tm, width), lambda i, c=col: (i, c))

  def mod_spec(self, which):
    d = self.d
    if self.time_major:
      return pl.BlockSpec((self.tm, d), lambda i, c=which: (0, c))
    tps = self.tiles_per_seq
    return pl.BlockSpec((None, 1, d), lambda i, c=which: (i // tps, 0, c))

  def mod_rows(self):
    return self.tm if self.time_major else 1


def _vec_spec(width):
  return pl.BlockSpec((1, width), lambda i: (0, 0))


def _full_spec(shape):
  nd = len(shape)
  return pl.BlockSpec(shape, lambda i, nd=nd: (0,) * nd)


def _row_params(blocks, temps=0):
  return pltpu.CompilerParams(
      dimension_semantics=("parallel",),
      vmem_limit_bytes=_vmem_limit(blocks, 0, temps))


def _rms(x, g):
  ms = jnp.mean(x * x, axis=-1, keepdims=True)
  return x * lax.rsqrt(ms + EPS_RMS) * g


def _prenorm_kernel(x_ref, g_ref, sc_ref, sh_ref, h_ref):
  h = _rms(x_ref[...], g_ref[...]) * (1.0 + sc_ref[...]) + sh_ref[...]
  h_ref[...] = h.astype(h_ref.dtype)


def _resid_kernel(x_ref, y_ref, gt_ref, gpost_ref, xo_ref):
  xo_ref[...] = x_ref[...] + gt_ref[...] * _rms(y_ref[...], gpost_ref[...])


def _resid_prenorm_kernel(x_ref, y_ref, gt_ref, gpost_ref, gpre_ref, sc_ref, sh_ref,
                          xo_ref, h_ref):
  xo = x_ref[...] + gt_ref[...] * _rms(y_ref[...], gpost_ref[...])
  xo_ref[...] = xo
  h = _rms(xo, gpre_ref[...]) * (1.0 + sc_ref[...]) + sh_ref[...]
  h_ref[...] = h.astype(h_ref.dtype)


def _prenorm(p, x, mod, g_pre, sc_idx, sh_idx):
  d = p.d
  blocks = _nbytes((p.tm, d), F32) + _nbytes((p.tm, d), BF16) + 3 * _nbytes((p.mod_rows(), d), F32)
  return pl.pallas_call(
      _prenorm_kernel,
      out_shape=jax.ShapeDtypeStruct((p.m, d), BF16),
      grid=(p.n_tiles,),
      in_specs=[p.row_spec(d), _vec_spec(d), p.mod_spec(sc_idx), p.mod_spec(sh_idx)],
      out_specs=p.row_spec(d),
      compiler_params=_row_params(blocks, 3 * _nbytes((p.tm, d), F32)),
      name=f"prenorm_{p.name}",
  )(x, g_pre.reshape(1, d), mod, mod)


def _resid(p, x, y, mod, gt_idx, g_post):
  d = p.d
  blocks = 3 * _nbytes((p.tm, d), F32) + 2 * _nbytes((p.mod_rows(), d), F32)
  return pl.pallas_call(
      _resid_kernel,
      out_shape=jax.ShapeDtypeStruct((p.m, d), F32),
      grid=(p.n_tiles,),
      in_specs=[p.row_spec(d), p.row_spec(d), p.mod_spec(gt_idx), _vec_spec(d)],
      out_specs=p.row_spec(d),
      compiler_params=_row_params(blocks, 3 * _nbytes((p.tm, d), F32)),
      name=f"resid_{p.name}",
  )(x, y, mod, g_post.reshape(1, d))


def _resid_prenorm(p, x, y, mod, gt_idx, g_post, mod_pre, g_pre, sc_idx, sh_idx):
  d = p.d
  blocks = (3 * _nbytes((p.tm, d), F32) + _nbytes((p.tm, d), BF16)
            + 5 * _nbytes((p.mod_rows(), d), F32))
  return pl.pallas_call(
      _resid_prenorm_kernel,
      out_shape=(jax.ShapeDtypeStruct((p.m, d), F32), jax.ShapeDtypeStruct((p.m, d), BF16)),
      grid=(p.n_tiles,),
      in_specs=[p.row_spec(d), p.row_spec(d), p.mod_spec(gt_idx), _vec_spec(d), _vec_spec(d),
                p.mod_spec(sc_idx), p.mod_spec(sh_idx)],
      out_specs=(p.row_spec(d), p.row_spec(d)),
      compiler_params=_row_params(blocks, 4 * _nbytes((p.tm, d), F32)),
      name=f"resid_prenorm_{p.name}",
  )(x, y, mod, g_post.reshape(1, d), g_pre.reshape(1, d), mod_pre, mod_pre)


def _gelu_exact(x):
  return 0.5 * x * (1.0 + lax.erf(x * (0.5 ** 0.5)))


def _layer_norm(x, w, b):
  mu = jnp.mean(x, axis=-1, keepdims=True)
  xc = x - mu
  var = jnp.mean(xc * xc, axis=-1, keepdims=True)
  return xc * lax.rsqrt(var + EPS_LN) * w + b


def _sgu_prompt_kernel(z_ref, lnw_ref, lnb_ref, ws_ref, bs_ref, oa_ref, *, d_a):
  dh = d_a // H_A
  ga = _gelu_exact(z_ref[...])
  u = ga[:, :d_a]
  van = _layer_norm(ga[:, d_a:], lnw_ref[...], lnb_ref[...]).astype(BF16)
  row = lax.broadcasted_iota(jnp.int32, (CHUNK, CHUNK), 0)
  col = lax.broadcasted_iota(jnp.int32, (CHUNK, CHUNK), 1)
  causal = row >= col
  for h in range(H_A):
    sl = slice(h * dh, (h + 1) * dh)
    w_h = jnp.where(causal, ws_ref[h], 0.0).astype(BF16)
    s_h = jnp.dot(w_h, van[:, sl], preferred_element_type=F32) + bs_ref[:, h:h + 1]
    oa_ref[:, sl] = (u[:, sl] * s_h).astype(oa_ref.dtype)


def _sgu_prompt(p, z, ln_w, ln_b, w_sp, b_sp, d_a):
  assert p.tm == CHUNK
  blocks = (_nbytes((CHUNK, 2 * d_a), F32) + _nbytes((H_A, CHUNK, CHUNK), F32)
            + _nbytes((CHUNK, d_a), BF16))
  return pl.pallas_call(
      functools.partial(_sgu_prompt_kernel, d_a=d_a),
      out_shape=jax.ShapeDtypeStruct((p.m, d_a), BF16),
      grid=(p.n_tiles,),
      in_specs=[p.row_spec(2 * d_a), _vec_spec(d_a), _vec_spec(d_a),
                _full_spec((H_A, CHUNK, CHUNK)), _full_spec((CHUNK, H_A))],
      out_specs=p.row_spec(d_a),
      compiler_params=_row_params(blocks, 4 * _nbytes((CHUNK, 2 * d_a), F32)),
      name=f"sgu_{p.name}",
  )(z, ln_w.reshape(1, d_a), ln_b.reshape(1, d_a), w_sp[:, :CHUNK, :CHUNK],
    b_sp[:, :CHUNK].T)


def _sgu_sample_kernel(z_ref, lnw_ref, lnb_ref, wsv_ref, bsv_ref, oa_ref, vn_ref, *, d_a, t, b):
  ga = _gelu_exact(z_ref[...])
  van = _layer_norm(ga[:, d_a:], lnw_ref[...], lnb_ref[...])
  vn_ref[...] = van
  for i in range(t):
    s = bsv_ref[i:i + 1, :]
    for j in range(i + 1):
      s = s + wsv_ref[i * t + j:i * t + j + 1, :] * van[j * b:(j + 1) * b, :]
    oa_ref[i * b:(i + 1) * b, :] = (ga[i * b:(i + 1) * b, :d_a] * s).astype(oa_ref.dtype)


def _sgu_sample(p, z, ln_w, ln_b, w_sp, b_sp, d_a):
  t, b = p.t, p.b
  dh = d_a // H_A
  wsv = jnp.repeat(w_sp[:, :t, :t].transpose(1, 2, 0).reshape(t * t, H_A), dh, axis=1)
  bsv = jnp.repeat(b_sp[:, :t].T, dh, axis=1)
  blocks = (_nbytes((p.m, 2 * d_a), F32) + _nbytes((p.m, d_a), BF16) + _nbytes((p.m, d_a), F32))
  return pl.pallas_call(
      functools.partial(_sgu_sample_kernel, d_a=d_a, t=t, b=b),
      out_shape=(jax.ShapeDtypeStruct((p.m, d_a), BF16), jax.ShapeDtypeStruct((p.m, d_a), F32)),
      grid=(1,),
      in_specs=[pl.BlockSpec((p.m, 2 * d_a), lambda i: (0, 0)), _vec_spec(d_a), _vec_spec(d_a),
                _full_spec((t * t, d_a)), _full_spec((t, d_a))],
      out_specs=(_full_spec((p.m, d_a)), _full_spec((p.m, d_a))),
      compiler_params=pltpu.CompilerParams(
          dimension_semantics=("arbitrary",),
          vmem_limit_bytes=_vmem_limit(blocks, 0, 3 * _nbytes((p.m, 2 * d_a), F32))),
      name=f"sgu_{p.name}",
  )(z, ln_w.reshape(1, d_a), ln_b.reshape(1, d_a), wsv, bsv)


def _rwkv_math(cur, delta, mu_ref, bias_ref, lora_refs, v_first):
  rc, kc, vc, xc = cur
  rd, kd, vd, xd = delta
  w1_ref, w2_ref, a1_ref, a2_ref, g1_ref, g2_ref, v1_ref, v2_ref = lora_refs

  def lora(x, m1_ref, m2_ref, mid=None):
    hcol = jnp.dot(x.astype(BF16), m1_ref[...], preferred_element_type=F32)
    if mid is not None:
      hcol = mid(hcol)
    return jnp.dot(hcol.astype(BF16), m2_ref[...], preferred_element_type=F32)

  mu = lambda i: mu_ref[i:i + 1, :]
  bias = lambda i: bias_ref[i:i + 1, :]
  lerp = lambda c, dl, m: c + dl * m

  r = lerp(rc, rd, mu(0))
  k = lerp(kc, kd, mu(1))
  v = lerp(vc, vd, mu(2))
  u = bias(0) + lora(lerp(xc, xd, mu(3)), w1_ref, w2_ref, jnp.tanh)
  w = jnp.exp(-jnp.exp(-jax.nn.softplus(-u) - 0.5))
  if v_first is not None:
    gate = jax.nn.sigmoid(bias(2) + lora(lerp(xc, xd, mu(6)), v1_ref, v2_ref))
    v = v + (v_first - v) * gate
  ag = jax.nn.sigmoid(bias(1) + lora(lerp(xc, xd, mu(4)), a1_ref, a2_ref))
  g = lora(lerp(xc, xd, mu(5)), g1_ref, g2_ref, jax.nn.sigmoid)
  return r, w, k, v, ag, g


def _rwkv_prep_sample_kernel(*refs, has_vfirst):
  cur_refs, prev_refs = refs[0:4], refs[4:8]
  mu_ref, bias_ref = refs[8:10]
  lora_refs = refs[10:18]
  pos = 18
  vf = None
  if has_vfirst:
    vf = refs[pos][...]
    pos += 1
  out_refs = refs[pos:pos + 6]
  cur = [c[...] for c in cur_refs]
  delta = [p[...] - c for p, c in zip(prev_refs, cur)]
  outs = _rwkv_math(cur, delta, mu_ref, bias_ref, lora_refs, vf)
  for o_ref, val in zip(out_refs, outs):
    o_ref[...] = val


def _rwkv_prep_sample(p, z, prev, mu7, bias3, loras, v_first, d_b, col0):
  has_vfirst = v_first is not None
  ins = [z] * 4 + [prev] * 4 + [mu7, bias3] + list(loras)
  specs = [p.row_spec(d_b, col0 + s) for s in range(4)] + [p.row_spec(d_b, s) for s in range(4)]
  specs += [_full_spec(mu7.shape), _full_spec(bias3.shape)] + [_full_spec(m.shape) for m in loras]
  if has_vfirst:
    ins.append(v_first)
    specs.append(p.row_spec(d_b))
  tile = _nbytes((p.tm, d_b), F32)
  out = jax.ShapeDtypeStruct((p.m, d_b), F32)
  return pl.pallas_call(
      functools.partial(_rwkv_prep_sample_kernel, has_vfirst=has_vfirst),
      out_shape=(out,) * 6,
      grid=(p.n_tiles,),
      in_specs=specs,
      out_specs=(p.row_spec(d_b),) * 6,
      compiler_params=_row_params(15 * tile + 8 * _nbytes(loras[0].shape, BF16), 10 * tile),
      name=f"rwkv_prep_{p.name}",
  )(*ins)


def _lane_parts(arrs, h_b):
  lane = lax.broadcasted_iota(jnp.int32, arrs[0].shape, 1)
  out = arrs[0]
  for k in range(1, len(arrs)):
    out = jnp.where(lane >= k * h_b, arrs[k], out)
  return out


def _interleave(vals, h_b):
  nb = len(vals)
  width = nb * h_b
  tiles = []
  for c in range(HEAD_B // nb):
    src = [v[:, c * width:(c + 1) * width] for v in vals]
    rolled = []
    for r in range(nb):
      merged = _lane_parts([src[(q + r) % nb] for q in range(nb)], h_b)
      rolled.append(merged if r == 0 else pltpu.roll(merged, r * h_b, axis=1))
    for q in range(nb):
      tiles.append(_lane_parts([rolled[(b - q) % nb] for b in range(nb)], h_b))
  return jnp.concatenate(tiles, axis=1)


def _deinterleave(row_ref, nb, h_b):
  width = nb * h_b
  cols = [[] for _ in range(nb)]
  for c in range(HEAD_B // nb):
    tiles = [row_ref[:, (nb * c + q) * width:(nb * c + q + 1) * width] for q in range(nb)]
    rolled = []
    for r in range(nb):
      merged = _lane_parts([tiles[(part + r) % nb] for part in range(nb)], h_b)
      rolled.append(merged if r == 0 else pltpu.roll(merged, r * h_b, axis=1))
    for b in range(nb):
      cols[b].append(_lane_parts([rolled[(q - b) % nb] for q in range(nb)], h_b))
  return [jnp.concatenate(cb, axis=1) for cb in cols]


def _rwkv_prep_prompt_kernel(*refs, nb, tm, h_b, has_vfirst, emit_v):
  cur_refs, shift0_ref = refs[0:4], refs[4]
  mu_ref, bias_ref = refs[5:7]
  lora_refs = refs[7:15]
  pos = 15
  vf = None
  rows = nb * tm
  if has_vfirst:
    vf = refs[pos][...].reshape(rows, -1)
    pos += 1
  row_out_refs = refs[pos:pos + 5]
  g_ref = refs[pos + 5]
  pos += 6
  if emit_v:
    v_tok_ref = refs[pos]
    pos += 1
  carry_ref = refs[pos]

  @pl.when(pl.program_id(0) == 0)
  def _():
    d_b = carry_ref.shape[2]
    for s in range(4):
      carry_ref[s] = shift0_ref[:, s * d_b:(s + 1) * d_b]

  cur, delta = [], []
  for s, c_ref in enumerate(cur_refs):
    c = c_ref[...].reshape(rows, -1)
    prev = pltpu.roll(c, 1, axis=0)
    row = lax.broadcasted_iota(jnp.int32, c.shape, 0)
    for b in range(nb):
      prev = jnp.where(row == b * tm, carry_ref[s, b:b + 1, :], prev)
    for b in range(nb):
      carry_ref[s, b:b + 1, :] = c_ref[b, tm - 1:tm, :]
    cur.append(c)
    delta.append(prev - c)
  r, w, k, v, ag, g = _rwkv_math(cur, delta, mu_ref, bias_ref, lora_refs, vf)
  for o_ref, val in zip(row_out_refs, (r, w, k, v, ag)):
    o_ref[...] = _interleave([val[b * tm:(b + 1) * tm] for b in range(nb)], h_b)
  g_ref[...] = g.reshape(nb, tm, -1)
  if emit_v:
    v_tok_ref[...] = v.reshape(nb, tm, -1)


def _rwkv_prep_prompt(p, zb, shift0, mu7, bias3, loras, v_first, d_b, emit_v):
  nb, t = p.b, p.t
  h_b = d_b // HEAD_B
  tm = _tile(t, 32, V7X_SUBLANES)
  has_vfirst = v_first is not None
  tok_spec = lambda c=0: pl.BlockSpec((nb, tm, d_b), lambda i, c=c: (0, i, c))
  ins = [zb] * 4 + [shift0, mu7, bias3] + list(loras)
  specs = [tok_spec(s) for s in range(4)]
  specs += [_full_spec(shift0.shape), _full_spec(mu7.shape), _full_spec(bias3.shape)]
  specs += [_full_spec(m.shape) for m in loras]
  if has_vfirst:
    ins.append(v_first)
    specs.append(tok_spec())
  width = HEAD_B * nb * h_b
  row_shape = jax.ShapeDtypeStruct((t, width), F32)
  tok_shape = jax.ShapeDtypeStruct((nb, t, d_b), F32)
  out_shape = (row_shape,) * 5 + (tok_shape,) + ((tok_shape,) if emit_v else ())
  out_specs = ((pl.BlockSpec((tm, width), lambda i: (i, 0)),) * 5 + (tok_spec(),)
               + ((tok_spec(),) if emit_v else ()))
  tile = _nbytes((nb * tm, d_b), F32)
  return pl.pallas_call(
      functools.partial(_rwkv_prep_prompt_kernel, nb=nb, tm=tm, h_b=h_b, has_vfirst=has_vfirst,
                        emit_v=emit_v),
      out_shape=out_shape,
      grid=(t // tm,),
      in_specs=specs,
      out_specs=out_specs,
      scratch_shapes=[pltpu.VMEM((4, nb, d_b), F32)],
      compiler_params=pltpu.CompilerParams(
          dimension_semantics=("arbitrary",),
          vmem_limit_bytes=_vmem_limit(12 * tile + 8 * _nbytes(loras[0].shape, BF16), 0,
                                       16 * tile)),
      name=f"rwkv_prep_{p.name}",
  )(*ins)


def _wkv_kernel(r_ref, w_ref, k_ref, v_ref, ag_ref, kk_ref, ka_ref, rk_ref, lnw_ref, lnb_ref,
                s0_ref, o_ref, s_ref, a_scr, b_scr, km_scr, yr_scr, vt_scr, yt_scr, *, tb, n, l):
  @pl.when(pl.program_id(1) == 0)
  def _():
    s_ref[...] = s0_ref[...]

  def tile_sum(x):
    acc = x[:, 0:l]
    for j in range(1, n):
      acc = acc + x[:, j * l:(j + 1) * l]
    return acc

  rep = lambda x: jnp.concatenate([x] * n, axis=1)
  row = lambda ref, u, j: ref[u:u + 1, j * l:(j + 1) * l]

  k_raw = k_ref[...]
  ag = ag_ref[...]
  kk = k_raw * kk_ref[...]
  kk = kk / rep(jnp.maximum(jnp.sqrt(tile_sum(kk * kk)), 1e-12))
  a_scr[0:tb, :] = -kk
  a_scr[tb:tb + V7X_SUBLANES, :] = jnp.zeros((V7X_SUBLANES, n * l), F32)
  b_scr[...] = kk * ag
  km_scr[...] = k_raw * (1.0 + (ag - 1.0) * ka_ref[...])

  sa0 = jnp.zeros((n, l), F32)
  for j in range(n):
    sa0 = sa0 + s_ref[j] * row(a_scr, 0, j)

  group = V7X_SUBLANES if tb % V7X_SUBLANES == 0 else tb

  def step_group(gi, sa):
    if tb == group:
      win = lambda ref, off=0: ref
      a_next = None
    else:
      base = pl.multiple_of(gi * group, group)
      win = lambda ref, off=0: ref.at[pl.ds(pl.multiple_of(base + off, group), group), :]
      a_next = win(a_scr, group)
    w_w, b_w, k_w, r_w, v_w, a_w, y_w = (win(w_ref), win(b_scr), win(km_scr), win(r_ref),
                                         win(v_ref), win(a_scr), win(yr_scr))
    for u in range(group):
      for i in range(n):
        vt_scr[i:i + 1, :] = row(v_w, u, i)
      v_t = vt_scr[...]
      y = jnp.zeros_like(sa)
      sa_next = jnp.zeros_like(sa)
      for j in range(n):
        s_new = s_ref[j] * row(w_w, u, j) + sa * row(b_w, u, j) + v_t * row(k_w, u, j)
        s_ref[j] = s_new
        y = y + s_new * row(r_w, u, j)
        if u + 1 < group or a_next is None:
          sa_next = sa_next + s_new * row(a_w, u + 1, j)
        else:
          sa_next = sa_next + s_new * row(a_next, 0, j)
      yt_scr[...] = y
      for i in range(n):
        y_w[u:u + 1, i * l:(i + 1) * l] = yt_scr[i:i + 1, :]
      sa = sa_next
    return sa

  if tb == group:
    step_group(0, sa0)
  else:
    lax.fori_loop(0, tb // group, step_group, sa0)

  y = yr_scr[...]
  yc = y - rep(tile_sum(y) * (1.0 / n))
  var = tile_sum(yc * yc) * (1.0 / n)
  yn = yc * rep(lax.rsqrt(var + EPS_GN)) * lnw_ref[...] + lnb_ref[...]
  rk = tile_sum(r_ref[...] * km_scr[...] * rk_ref[...])
  o_ref[...] = yn + rep(rk) * v_ref[...]


def _wkv(seq, params, s0, *, tb, name):
  g, t, width = seq[0].shape
  n = HEAD_B
  l = width // n
  tb = _tile(t, tb, V7X_SUBLANES)
  seq_spec = pl.BlockSpec((None, tb, width), lambda gi, ti: (gi, ti, 0))
  par_spec = pl.BlockSpec((None, 1, width), lambda gi, ti: (gi, 0, 0))
  st_spec = pl.BlockSpec((None, n, n, l), lambda gi, ti: (gi, 0, 0, 0))
  blk = _nbytes((tb, width), F32)
  st = _nbytes((n, n, l), F32)
  return pl.pallas_call(
      functools.partial(_wkv_kernel, tb=tb, n=n, l=l),
      out_shape=(jax.ShapeDtypeStruct((g, t, width), F32), jax.ShapeDtypeStruct((g, n, n, l), F32)),
      grid=(g, t // tb),
      in_specs=[seq_spec] * 5 + [par_spec] * 5 + [st_spec],
      out_specs=(seq_spec, st_spec),
      scratch_shapes=[pltpu.VMEM((tb + V7X_SUBLANES, width), F32), pltpu.VMEM((tb, width), F32),
                      pltpu.VMEM((tb, width), F32), pltpu.VMEM((tb, width), F32),
                      pltpu.VMEM((n, l), F32), pltpu.VMEM((n, l), F32)],
      compiler_params=pltpu.CompilerParams(
          dimension_semantics=("parallel", "arbitrary"),
          vmem_limit_bytes=_vmem_limit(6 * blk + 2 * st + 5 * _nbytes((1, width), F32),
                                       6 * blk, 10 * blk)),
      name=name,
  )(*seq, *params, s0)


def _cat_sample_kernel(oa_ref, ob_ref, g_ref, o_ref, *, d_a):
  o_ref[:, :d_a] = oa_ref[...]
  o_ref[:, d_a:] = (ob_ref[...] * g_ref[...]).astype(o_ref.dtype)


def _cat_sample(p, oa, ob, g, d_a, d_b):
  d = d_a + d_b
  blocks = _nbytes((p.tm, d_a), BF16) + 2 * _nbytes((p.tm, d_b), F32) + _nbytes((p.tm, d), BF16)
  return pl.pallas_call(
      functools.partial(_cat_sample_kernel, d_a=d_a),
      out_shape=jax.ShapeDtypeStruct((p.m, d), BF16),
      grid=(p.n_tiles,),
      in_specs=[p.row_spec(d_a), p.row_spec(d_b), p.row_spec(d_b)],
      out_specs=p.row_spec(d),
      compiler_params=_row_params(blocks, _nbytes((p.tm, d_b), F32)),
      name=f"cat_{p.name}",
  )(oa, ob, g)


def _cat_prompt_kernel(oa_ref, ob_ref, g_ref, o_ref, *, nb, h_b, d_a):
  o_ref[:, :, :d_a] = oa_ref[...]
  for b, ob in enumerate(_deinterleave(ob_ref, nb, h_b)):
    o_ref[b, :, d_a:] = (ob * g_ref[b]).astype(o_ref.dtype)


def _cat_prompt(p, oa, ob, g, d_a, d_b):
  nb, t = p.b, p.t
  h_b = d_b // HEAD_B
  d = d_a + d_b
  tm = _tile(t, 64, V7X_SUBLANES)
  width = ob.shape[1]
  blocks = (_nbytes((nb, tm, d_a), BF16) + _nbytes((tm, width), F32) + _nbytes((nb, tm, d_b), F32)
            + _nbytes((nb, tm, d), BF16))
  return pl.pallas_call(
      functools.partial(_cat_prompt_kernel, nb=nb, h_b=h_b, d_a=d_a),
      out_shape=jax.ShapeDtypeStruct((nb, t, d), BF16),
      grid=(t // tm,),
      in_specs=[pl.BlockSpec((nb, tm, d_a), lambda i: (0, i, 0)),
                pl.BlockSpec((tm, width), lambda i: (i, 0)),
                pl.BlockSpec((nb, tm, d_b), lambda i: (0, i, 0))],
      out_specs=pl.BlockSpec((nb, tm, d), lambda i: (0, i, 0)),
      compiler_params=_row_params(blocks, 3 * _nbytes((tm, width), F32)),
      name=f"cat_{p.name}",
  )(oa, ob, g)


def _to_jh(x, h_b):
  s = x.shape[:-1]
  return x.reshape(s + (h_b, HEAD_B)).swapaxes(-1, -2).reshape(s + (h_b * HEAD_B,))


def _from_jh(x, h_b):
  s = x.shape[:-1]
  return x.reshape(s + (HEAD_B, h_b)).swapaxes(-1, -2).reshape(s + (h_b * HEAD_B,))


def _slabs_to_jh(x, h_b):
  s = x.shape[:-1]
  return _to_jh(x.reshape(s + (4, h_b * HEAD_B)), h_b).reshape(s + (4 * h_b * HEAD_B,))


def _slabs_from_jh(x, h_b):
  s = x.shape[:-1]
  return _from_jh(x.reshape(s + (4, h_b * HEAD_B)), h_b).reshape(s + (4 * h_b * HEAD_B,))


def _scan_param(p, v, h_b):
  vh = v.reshape(h_b, HEAD_B)
  if p.time_major:
    return jnp.repeat(vh, p.b, axis=1).reshape(h_b, 1, HEAD_B * p.b)
  return jnp.tile(vh.T, (1, p.b)).reshape(1, 1, HEAD_B * p.b * h_b)


def _state_to_scan(p, s):
  if p.time_major:
    return s.transpose(1, 3, 2, 0)
  b, h_b = s.shape[:2]
  return s.transpose(3, 2, 0, 1).reshape(1, HEAD_B, HEAD_B, b * h_b)


def _state_from_scan(p, s, h_b):
  if p.time_major:
    return s.transpose(3, 0, 2, 1)
  return s.reshape(HEAD_B, HEAD_B, p.b, h_b).transpose(2, 3, 1, 0)


def _trunk(p, x, mod_l, shift0, wkv0, weights, keep_chunk_rows):
  (g_pre_mix, g_post_mix, g_pre_ffn, g_post_ffn, w_in_a, w_in_b, w_out, ln_v_w, ln_v_b, w_spatial,
   b_spatial, mu7s, bias3s, loras, k_k, k_a, r_k, ln_x_w, ln_x_b, w_up, w_down) = weights
  depth = w_out.shape[0]
  d = p.d
  d_a = d // 2
  d_b = d - d_a
  h_b = d_b // HEAD_B
  mm_tm = 1024 if not p.time_major else 512
  new_wkv, new_shift, chunk_rows = [], [], []
  v_first = None
  h = _prenorm(p, x, mod_l[0], g_pre_mix[0], 1, 0)
  for l in range(depth):
    mod = mod_l[l]
    za = _matmul(h, w_in_a, l, F32, tm=mm_tm, name=f"mm_in_a_{p.name}")
    zb = _matmul(h, w_in_b, l, F32, tm=mm_tm, name=f"mm_in_b_{p.name}")
    par = [_scan_param(p, t, h_b) for t in (k_k[l], k_a[l], r_k[l].reshape(-1), ln_x_w[l], ln_x_b[l])]
    s0 = _state_to_scan(p, wkv0[l])
    if p.time_major:
      out_a, va_n = _sgu_sample(p, za, ln_v_w[l], ln_v_b[l], w_spatial[l], b_spatial[l], d_a)
      if keep_chunk_rows:
        chunk_rows.append(va_n)
      prev = jnp.concatenate([shift0[l], zb[:-p.b]], axis=0)
      new_shift.append(zb[(p.t - 1) * p.b:])
      r, w, k, v, ag, g = _rwkv_prep_sample(p, zb, prev, mu7s[l], bias3s[l], loras[l], v_first,
                                            d_b, 0)
      if l == 0:
        v_first = v
      to_scan = lambda a: (a.reshape(p.t, p.b, HEAD_B, h_b).transpose(3, 0, 2, 1)
                           .reshape(h_b, p.t, HEAD_B * p.b))
      ob, s_new = _wkv([to_scan(a) for a in (r, w, k, v, ag)], par, s0, tb=32,
                       name=f"wkv_{p.name}")
      ob = ob.reshape(h_b, p.t, HEAD_B, p.b).transpose(1, 3, 2, 0).reshape(p.m, d_b)
      cat = _cat_sample(p, out_a, ob, g, d_a, d_b)
    else:
      out_a = _sgu_prompt(p, za, ln_v_w[l], ln_v_b[l], w_spatial[l], b_spatial[l], d_a)
      zb3 = zb.reshape(p.b, p.t, 4 * d_b)
      new_shift.append(zb3[:, p.t - 1])
      outs = _rwkv_prep_prompt(p, zb3, shift0[l], mu7s[l], bias3s[l], loras[l], v_first, d_b,
                               emit_v=(l == 0))
      if l == 0:
        v_first = outs[6]
      seq = [a[None] for a in outs[:5]]
      ob, s_new = _wkv(seq, par, s0, tb=32, name=f"wkv_{p.name}")
      cat = _cat_prompt(p, out_a.reshape(p.b, p.t, d_a), ob[0], outs[5], d_a, d_b)
      cat = cat.reshape(p.m, d)
    new_wkv.append(_state_from_scan(p, s_new, h_b))
    y = _matmul(cat, w_out, l, F32, tm=mm_tm, name=f"mm_out_{p.name}")
    x, h = _resid_prenorm(p, x, y, mod, 2, g_post_mix[l], mod, g_pre_ffn[l], 4, 3)
    f = _matmul(h, w_up, l, BF16, act=_relu2, tm=mm_tm, name=f"mm_up_{p.name}")
    f = _matmul(f, w_down, l, F32, tm=mm_tm, name=f"mm_down_{p.name}")
    if l + 1 < depth:
      x, h = _resid_prenorm(p, x, f, mod, 5, g_post_ffn[l], mod_l[l + 1], g_pre_mix[l + 1], 1, 0)
    else:
      x = _resid(p, x, f, mod, 5, g_post_ffn[l])
  rows = jnp.stack(chunk_rows) if keep_chunk_rows else None
  return x, jnp.stack(new_wkv), jnp.stack(new_shift), rows


def _pad_lora(m1, m2, h_b):
  rank = m1.shape[1]
  rp = -(-rank // V7X_LANES) * V7X_LANES
  m1 = jnp.pad(_to_jh(m1.T, h_b).T, ((0, 0), (0, rp - rank))).astype(BF16)
  m2 = jnp.pad(_to_jh(m2, h_b), ((0, rp - rank), (0, 0))).astype(BF16)
  return m1, m2


def kernel(x_prompt, x_sample, state_wkv, state_shift, c_prompt, c_sample, w_ada, b_ada, g_pre_mix, g_post_mix, g_pre_ffn, g_post_ffn, w_in, w_out, ln_v_w, ln_v_b, w_spatial, b_spatial, mu_rkv, mu_lora, mu_vm, w0, w1, w2, a0, a1, a2, v0, v1, v2, g1, g2, k_k, k_a, r_k, ln_x_w, ln_x_b, w_up, w_down):
  bp, tp, d = x_prompt.shape
  bs, ts, _ = x_sample.shape
  depth = w_in.shape[0]
  d_a = d // 2
  d_b = d - d_a
  h_b = d_b // HEAD_B
  p_a = 2 * d_a

  n_c = bs + bp
  c_all = jnp.concatenate([c_sample, c_prompt, jnp.zeros((-n_c % V7X_SUBLANES, d), F32)], axis=0)
  mod_all = _ada_all(c_all, w_ada, b_ada)
  mod_s = [mod_all[l, :bs] for l in range(depth)]
  mod_p = [mod_all[l, bs:n_c].reshape(bp, 1, N_MOD * d) for l in range(depth)]

  w_in_a = w_in[..., :p_a].astype(BF16)
  w_in_b = _slabs_to_jh(w_in[..., p_a:], h_b).astype(BF16)
  w_out_b = w_out[:, d_a:].reshape(depth, h_b, HEAD_B, d).swapaxes(1, 2).reshape(depth, d_b, d)
  w_out_p = jnp.concatenate([w_out[:, :d_a], w_out_b], axis=1).astype(BF16)
  loras, mu7s, bias3s = [], [], []
  for l in range(depth):
    lv = max(l - 1, 0)
    loras.append(_pad_lora(w1[l], w2[l], h_b) + _pad_lora(a1[l], a2[l], h_b)
                 + _pad_lora(g1[l], g2[l], h_b) + _pad_lora(v1[lv], v2[lv], h_b))
    mu7s.append(_to_jh(jnp.concatenate([mu_rkv[l], mu_lora[l], mu_vm[lv][None]], axis=0), h_b))
    bias3s.append(_to_jh(jnp.stack([w0[l], a0[l], v0[lv]]), h_b))
  weights = (g_pre_mix, g_post_mix, g_pre_ffn, g_post_ffn, w_in_a, w_in_b, w_out_p, ln_v_w, ln_v_b,
             w_spatial, b_spatial, mu7s, bias3s, loras, k_k, k_a, r_k, ln_x_w, ln_x_b,
             w_up.astype(BF16), w_down.astype(BF16))

  pp = _Pass("prompt", bp, tp, d, time_major=False)
  shift0 = jnp.zeros((depth, bp, 4 * d_b), x_prompt.dtype)
  wkv0 = jnp.zeros((depth, bp, h_b, HEAD_B, HEAD_B), state_wkv.dtype)
  y_p, wkv_p, shift_p, _ = _trunk(pp, x_prompt.reshape(bp * tp, d), mod_p, shift0, wkv0,
                                  weights, False)

  ps = _Pass("sample", bs, ts, d, time_major=True)
  x_s = x_sample.transpose(1, 0, 2).reshape(ts * bs, d)
  y_s, wkv_s, shift_s, rows_s = _trunk(ps, x_s, mod_s, _slabs_to_jh(state_shift, h_b), state_wkv,
                                       weights, True)

  y_sample = y_s.reshape(ts, bs, d).transpose(1, 0, 2)
  chunk_v = rows_s.reshape(depth, ts, bs, d_a).transpose(0, 2, 1, 3)
  return (y_p.reshape(bp, tp, d), y_sample, wkv_p, _slabs_from_jh(shift_p, h_b), wkv_s,
          _slabs_from_jh(shift_s, h_b), chunk_v)
```

```python
import functools

import jax
import jax.numpy as jnp
from jax import lax
from jax.experimental import pallas as pl
from jax.experimental.pallas import tpu as pltpu

F32 = jnp.float32
BF16 = jnp.bfloat16

H_A = 8
CHUNK = 128
HEAD_B = 64
N_MOD = 6
EPS_RMS = 1e-6
EPS_LN = 1e-5
EPS_GN = 64e-5

V7X_LANES = 128
V7X_SUBLANES = 8
VMEM_LIMIT_CAP = 56 * 1024 * 1024


def _vmem_limit(block_bytes, scratch_bytes=0, temp_bytes=0):
  est = 2 * block_bytes + scratch_bytes + temp_bytes + (2 << 20)
  return int(min(max(est, 16 << 20), VMEM_LIMIT_CAP))


def _tile(dim, target, align=V7X_LANES):
  if dim <= target:
    return dim
  t = (target // align) * align
  while t >= align:
    if dim % t == 0:
      return t
    t -= align
  return dim


def _nbytes(shape, dtype):
  n = 1
  for s in shape:
    n *= s
  return n * jnp.dtype(dtype).itemsize


def _relu2(x):
  r = jnp.maximum(x, 0.0)
  return r * r


def _mm_kernel(a_ref, b_ref, o_ref, *scratch, nk, act):
  if nk == 1:
    acc = jnp.dot(a_ref[...], b_ref[...], preferred_element_type=F32)
    o_ref[...] = (act(acc) if act else acc).astype(o_ref.dtype)
    return
  k = pl.program_id(2)
  if not scratch:
    @pl.when(k == 0)
    def _():
      o_ref[...] = jnp.dot(a_ref[...], b_ref[...], preferred_element_type=F32)

    @pl.when(k > 0)
    def _():
      o_ref[...] += jnp.dot(a_ref[...], b_ref[...], preferred_element_type=F32)
    return
  (acc_ref,) = scratch

  @pl.when(k == 0)
  def _():
    acc_ref[...] = jnp.zeros_like(acc_ref)

  acc_ref[...] += jnp.dot(a_ref[...], b_ref[...], preferred_element_type=F32)

  @pl.when(k == nk - 1)
  def _():
    acc = acc_ref[...]
    o_ref[...] = (act(acc) if act else acc).astype(o_ref.dtype)


def _matmul(a, w, layer, out_dtype, *, act=None, tm=1024, tn=1024, tk=4096, name="matmul"):
  m, kdim = a.shape
  n = w.shape[2]
  tm, tn, tk = _tile(m, tm, V7X_SUBLANES), _tile(n, tn), _tile(kdim, tk)
  nk = kdim // tk
  blocks = (_nbytes((tm, tk), a.dtype) + _nbytes((tk, tn), w.dtype)
            + _nbytes((tm, tn), out_dtype))
  acc_bytes = _nbytes((tm, tn), F32)
  need_acc = nk > 1 and (act is not None or jnp.dtype(out_dtype) != jnp.dtype(F32))
  return pl.pallas_call(
      functools.partial(_mm_kernel, nk=nk, act=act),
      out_shape=jax.ShapeDtypeStruct((m, n), out_dtype),
      grid=(m // tm, n // tn, nk),
      in_specs=[pl.BlockSpec((tm, tk), lambda i, j, k: (i, k)),
                pl.BlockSpec((None, tk, tn), lambda i, j, k: (layer, k, j))],
      out_specs=pl.BlockSpec((tm, tn), lambda i, j, k: (i, j)),
      scratch_shapes=[pltpu.VMEM((tm, tn), F32)] if need_acc else [],
      compiler_params=pltpu.CompilerParams(
          dimension_semantics=("parallel", "parallel", "arbitrary"),
          vmem_limit_bytes=_vmem_limit(blocks, acc_bytes if need_acc else 0, 2 * acc_bytes)),
      name=name,
  )(a, w)


def _mm_f32w_kernel(a_ref, b_ref, o_ref, wb_ref, *, act):
  @pl.when(pl.program_id(1) == 0)
  def _():
    wb_ref[...] = b_ref[...].astype(BF16)

  acc = jnp.dot(a_ref[...], wb_ref[...], preferred_element_type=F32)
  o_ref[...] = (act(acc) if act else acc).astype(o_ref.dtype)


def _matmul_f32w(a, w, layer, n_cols, out_dtype, *, act=None, tm=1024, tn=512, name="matmul"):
  m, kdim = a.shape
  tm, tn = _tile(m, tm, V7X_SUBLANES), _tile(n_cols, tn)
  blocks = (_nbytes((tm, kdim), a.dtype) + _nbytes((kdim, tn), F32) + _nbytes((tm, tn), out_dtype))
  return pl.pallas_call(
      functools.partial(_mm_f32w_kernel, act=act),
      out_shape=jax.ShapeDtypeStruct((m, n_cols), out_dtype),
      grid=(n_cols // tn, m // tm),
      in_specs=[pl.BlockSpec((tm, kdim), lambda j, i: (i, 0)),
                pl.BlockSpec((None, kdim, tn), lambda j, i: (layer, 0, j))],
      out_specs=pl.BlockSpec((tm, tn), lambda j, i: (i, j)),
      scratch_shapes=[pltpu.VMEM((kdim, tn), BF16)],
      compiler_params=pltpu.CompilerParams(
          dimension_semantics=("arbitrary", "arbitrary"),
          vmem_limit_bytes=_vmem_limit(blocks, _nbytes((kdim, tn), BF16),
                                       2 * _nbytes((tm, tn), F32))),
      name=name,
  )(a, w)


def _mm_pair_kernel(a1_ref, a2_ref, b1_ref, b2_ref, o_ref):
  acc = jnp.dot(a1_ref[...], b1_ref[...], preferred_element_type=F32)
  acc = acc + jnp.dot(a2_ref[...], b2_ref[...], preferred_element_type=F32)
  o_ref[...] = acc.astype(o_ref.dtype)


def _matmul_pair(a, k1, w1, w2, layer, out_dtype, *, tm=1024, tn=1024, name="matmul"):
  m, kdim = a.shape
  k2 = kdim - k1
  assert k1 == k2 and w1.shape[1] == k1 and w2.shape[1] == k2
  n = w1.shape[2]
  tm, tn = _tile(m, tm, V7X_SUBLANES), _tile(n, tn)
  blocks = (_nbytes((tm, kdim), a.dtype) + _nbytes((kdim, tn), w1.dtype)
            + _nbytes((tm, tn), out_dtype))
  return pl.pallas_call(
      _mm_pair_kernel,
      out_shape=jax.ShapeDtypeStruct((m, n), out_dtype),
      grid=(m // tm, n // tn),
      in_specs=[pl.BlockSpec((tm, k1), lambda i, j: (i, 0)),
                pl.BlockSpec((tm, k2), lambda i, j: (i, 1)),
                pl.BlockSpec((None, k1, tn), lambda i, j: (layer, 0, j)),
                pl.BlockSpec((None, k2, tn), lambda i, j: (layer, 0, j))],
      out_specs=pl.BlockSpec((tm, tn), lambda i, j: (i, j)),
      compiler_params=pltpu.CompilerParams(
          dimension_semantics=("parallel", "parallel"),
          vmem_limit_bytes=_vmem_limit(blocks, 0, 2 * _nbytes((tm, tn), F32))),
      name=name,
  )(a, a, w1, w2)


def _ada_kernel(c_ref, w_ref, b_ref, o_ref, acc_ref, *, nk):
  k = pl.program_id(2)

  @pl.when(k == 0)
  def _():
    acc_ref[...] = jnp.zeros_like(acc_ref)

  c = c_ref[...]
  c_act = (c * jax.nn.sigmoid(c)).astype(BF16)
  acc_ref[...] += jnp.dot(c_act, w_ref[...].astype(BF16), preferred_element_type=F32)

  @pl.when(k == nk - 1)
  def _():
    o_ref[...] = acc_ref[...] + b_ref[...]


def _ada_all(c_all, w_ada, b_ada):
  depth, d, n = w_ada.shape
  mp = c_all.shape[0]
  tk, tn = _tile(d, 1024), _tile(n, 2048)
  nk = d // tk
  blocks = (_nbytes((mp, tk), F32) + _nbytes((tk, tn), F32) + _nbytes((1, tn), F32)
            + _nbytes((mp, tn), F32))
  return pl.pallas_call(
      functools.partial(_ada_kernel, nk=nk),
      out_shape=jax.ShapeDtypeStruct((depth, mp, n), F32),
      grid=(depth, n // tn, nk),
      in_specs=[pl.BlockSpec((mp, tk), lambda l, j, k: (0, k)),
                pl.BlockSpec((None, tk, tn), lambda l, j, k: (l, k, j)),
                pl.BlockSpec((None, 1, tn), lambda l, j, k: (l, 0, j))],
      out_specs=pl.BlockSpec((None, mp, tn), lambda l, j, k: (l, 0, j)),
      scratch_shapes=[pltpu.VMEM((mp, tn), F32)],
      compiler_params=pltpu.CompilerParams(
          dimension_semantics=("parallel", "parallel", "arbitrary"),
          vmem_limit_bytes=_vmem_limit(blocks, _nbytes((mp, tn), F32),
                                       _nbytes((tk, tn), BF16))),
      name="ada_mod",
  )(c_all, w_ada, b_ada.reshape(depth, 1, n))


class _Pass:
  def __init__(self, name, b, t, d, time_major):
    self.name, self.b, self.t, self.d, self.time_major = name, b, t, d, time_major
    self.m = b * t
    self.tm = b if time_major else _tile(t, CHUNK, V7X_SUBLANES)
    self.n_tiles = self.m // self.tm
    self.tiles_per_seq = 1 if time_major else t // self.tm

  def row_spec(self, width, col=0):
    return pl.BlockSpec((self.tm, width), lambda i, c=col: (i, c))

  def mod_spec(self, which):
    d = self.d
    if self.time_major:
      return pl.BlockSpec((self.tm, d), lambda i, c=which: (0, c))
    tps = self.tiles_per_seq
    return pl.BlockSpec((None, 1, d), lambda i, c=which: (i // tps, 0, c))

  def mod_rows(self):
    return self.tm if self.time_major else 1


def _vec_spec(width):
  return pl.BlockSpec((1, width), lambda i: (0, 0))


def _full_spec(shape):
  nd = len(shape)
  return pl.BlockSpec(shape, lambda i, nd=nd: (0,) * nd)


def _row_params(blocks, temps=0):
  return pltpu.CompilerParams(
      dimension_semantics=("parallel",),
      vmem_limit_bytes=_vmem_limit(blocks, 0, temps))


def _rms(x, g):
  ms = jnp.mean(x * x, axis=-1, keepdims=True)
  return x * lax.rsqrt(ms + EPS_RMS) * g


def _prenorm_kernel(x_ref, g_ref, sc_ref, sh_ref, h_ref):
  h = _rms(x_ref[...], g_ref[...]) * (1.0 + sc_ref[...]) + sh_ref[...]
  h_ref[...] = h.astype(h_ref.dtype)


def _resid_kernel(x_ref, y_ref, gt_ref, gpost_ref, xo_ref):
  xo_ref[...] = x_ref[...] + gt_ref[...] * _rms(y_ref[...], gpost_ref[...])


def _resid_prenorm_kernel(x_ref, y_ref, gt_ref, gpost_ref, gpre_ref, sc_ref, sh_ref,
                          xo_ref, h_ref):
  xo = x_ref[...] + gt_ref[...] * _rms(y_ref[...], gpost_ref[...])
  xo_ref[...] = xo
  h = _rms(xo, gpre_ref[...]) * (1.0 + sc_ref[...]) + sh_ref[...]
  h_ref[...] = h.astype(h_ref.dtype)


def _prenorm(p, x, mod, g_pre, sc_idx, sh_idx):
  d = p.d
  blocks = _nbytes((p.tm, d), F32) + _nbytes((p.tm, d), BF16) + 3 * _nbytes((p.mod_rows(), d), F32)
  return pl.pallas_call(
      _prenorm_kernel,
      out_shape=jax.ShapeDtypeStruct((p.m, d), BF16),
      grid=(p.n_tiles,),
      in_specs=[p.row_spec(d), _vec_spec(d), p.mod_spec(sc_idx), p.mod_spec(sh_idx)],
      out_specs=p.row_spec(d),
      compiler_params=_row_params(blocks, 3 * _nbytes((p.tm, d), F32)),
      name=f"prenorm_{p.name}",
  )(x, g_pre.reshape(1, d), mod, mod)


def _resid(p, x, y, mod, gt_idx, g_post):
  d = p.d
  blocks = 3 * _nbytes((p.tm, d), F32) + 2 * _nbytes((p.mod_rows(), d), F32)
  return pl.pallas_call(
      _resid_kernel,
      out_shape=jax.ShapeDtypeStruct((p.m, d), F32),
      grid=(p.n_tiles,),
      in_specs=[p.row_spec(d), p.row_spec(d), p.mod_spec(gt_idx), _vec_spec(d)],
      out_specs=p.row_spec(d),
      compiler_params=_row_params(blocks, 3 * _nbytes((p.tm, d), F32)),
      name=f"resid_{p.name}",
  )(x, y, mod, g_post.reshape(1, d))


def _resid_prenorm(p, x, y, mod, gt_idx, g_post, mod_pre, g_pre, sc_idx, sh_idx):
  d = p.d
  blocks = (3 * _nbytes((p.tm, d), F32) + _nbytes((p.tm, d), BF16)
            + 5 * _nbytes((p.mod_rows(), d), F32))
  return pl.pallas_call(
      _resid_prenorm_kernel,
      out_shape=(jax.ShapeDtypeStruct((p.m, d), F32), jax.ShapeDtypeStruct((p.m, d), BF16)),
      grid=(p.n_tiles,),
      in_specs=[p.row_spec(d), p.row_spec(d), p.mod_spec(gt_idx), _vec_spec(d), _vec_spec(d),
                p.mod_spec(sc_idx), p.mod_spec(sh_idx)],
      out_specs=(p.row_spec(d), p.row_spec(d)),
      compiler_params=_row_params(blocks, 4 * _nbytes((p.tm, d), F32)),
      name=f"resid_prenorm_{p.name}",
  )(x, y, mod, g_post.reshape(1, d), g_pre.reshape(1, d), mod_pre, mod_pre)


def _gelu_exact(x):
  return 0.5 * x * (1.0 + lax.erf(x * (0.5 ** 0.5)))


def _layer_norm(x, w, b):
  mu = jnp.mean(x, axis=-1, keepdims=True)
  xc = x - mu
  var = jnp.mean(xc * xc, axis=-1, keepdims=True)
  return xc * lax.rsqrt(var + EPS_LN) * w + b


def _sgu_prompt_kernel(z_ref, lnw_ref, lnb_ref, ws_ref, bs_ref, oa_ref, *, d_a):
  dh = d_a // H_A
  ga = _gelu_exact(z_ref[...])
  u = ga[:, :d_a]
  van = _layer_norm(ga[:, d_a:], lnw_ref[...], lnb_ref[...]).astype(BF16)
  row = lax.broadcasted_iota(jnp.int32, (CHUNK, CHUNK), 0)
  col = lax.broadcasted_iota(jnp.int32, (CHUNK, CHUNK), 1)
  causal = row >= col
  for h in range(H_A):
    sl = slice(h * dh, (h + 1) * dh)
    w_h = jnp.where(causal, ws_ref[h], 0.0).astype(BF16)
    s_h = jnp.dot(w_h, van[:, sl], preferred_element_type=F32) + bs_ref[:, h:h + 1]
    oa_ref[:, sl] = (u[:, sl] * s_h).astype(oa_ref.dtype)


def _sgu_prompt(p, z, ln_w, ln_b, w_sp, b_sp, d_a):
  assert p.tm == CHUNK
  blocks = (_nbytes((CHUNK, 2 * d_a), F32) + _nbytes((H_A, CHUNK, CHUNK), F32)
            + _nbytes((CHUNK, d_a), BF16))
  return pl.pallas_call(
      functools.partial(_sgu_prompt_kernel, d_a=d_a),
      out_shape=jax.ShapeDtypeStruct((p.m, d_a), BF16),
      grid=(p.n_tiles,),
      in_specs=[p.row_spec(2 * d_a), _vec_spec(d_a), _vec_spec(d_a),
                _full_spec((H_A, CHUNK, CHUNK)), _full_spec((CHUNK, H_A))],
      out_specs=p.row_spec(d_a),
      compiler_params=_row_params(blocks, 4 * _nbytes((CHUNK, 2 * d_a), F32)),
      name=f"sgu_{p.name}",
  )(z, ln_w.reshape(1, d_a), ln_b.reshape(1, d_a), w_sp[:, :CHUNK, :CHUNK],
    b_sp[:, :CHUNK].T)


def _sgu_sample_kernel(z_ref, lnw_ref, lnb_ref, wsv_ref, bsv_ref, oa_ref, vn_ref, *, d_a, t, b):
  ga = _gelu_exact(z_ref[...])
  van = _layer_norm(ga[:, d_a:], lnw_ref[...], lnb_ref[...])
  vn_ref[...] = van
  for i in range(t):
    s = bsv_ref[i:i + 1, :]
    for j in range(i + 1):
      s = s + wsv_ref[i * t + j:i * t + j + 1, :] * van[j * b:(j + 1) * b, :]
    oa_ref[i * b:(i + 1) * b, :] = (ga[i * b:(i + 1) * b, :d_a] * s).astype(oa_ref.dtype)


def _sgu_sample(p, z, ln_w, ln_b, w_sp, b_sp, d_a):
  t, b = p.t, p.b
  dh = d_a // H_A
  wsv = jnp.repeat(w_sp[:, :t, :t].transpose(1, 2, 0).reshape(t * t, H_A), dh, axis=1)
  bsv = jnp.repeat(b_sp[:, :t].T, dh, axis=1)
  blocks = (_nbytes((p.m, 2 * d_a), F32) + _nbytes((p.m, d_a), BF16) + _nbytes((p.m, d_a), F32))
  return pl.pallas_call(
      functools.partial(_sgu_sample_kernel, d_a=d_a, t=t, b=b),
      out_shape=(jax.ShapeDtypeStruct((p.m, d_a), BF16), jax.ShapeDtypeStruct((p.m, d_a), F32)),
      grid=(1,),
      in_specs=[pl.BlockSpec((p.m, 2 * d_a), lambda i: (0, 0)), _vec_spec(d_a), _vec_spec(d_a),
                _full_spec((t * t, d_a)), _full_spec((t, d_a))],
      out_specs=(_full_spec((p.m, d_a)), _full_spec((p.m, d_a))),
      compiler_params=pltpu.CompilerParams(
          dimension_semantics=("arbitrary",),
          vmem_limit_bytes=_vmem_limit(blocks, 0, 3 * _nbytes((p.m, 2 * d_a), F32))),
      name=f"sgu_{p.name}",
  )(z, ln_w.reshape(1, d_a), ln_b.reshape(1, d_a), wsv, bsv)


def _rwkv_math(cur, delta, mu_ref, bias_ref, lora_refs, v_first):
  rc, kc, vc, xc = cur
  rd, kd, vd, xd = delta
  w1_ref, w2_ref, a1_ref, a2_ref, g1_ref, g2_ref, v1_ref, v2_ref = lora_refs

  def lora(x, m1_ref, m2_ref, mid=None):
    hcol = jnp.dot(x.astype(BF16), m1_ref[...], preferred_element_type=F32)
    if mid is not None:
      hcol = mid(hcol)
    return jnp.dot(hcol.astype(BF16), m2_ref[...], preferred_element_type=F32)

  mu = lambda i: mu_ref[i:i + 1, :]
  bias = lambda i: bias_ref[i:i + 1, :]
  lerp = lambda c, dl, m: c + dl * m

  r = lerp(rc, rd, mu(0))
  k = lerp(kc, kd, mu(1))
  v = lerp(vc, vd, mu(2))
  u = bias(0) + lora(lerp(xc, xd, mu(3)), w1_ref, w2_ref, jnp.tanh)
  w = jnp.exp(-jnp.exp(-jax.nn.softplus(-u) - 0.5))
  if v_first is not None:
    gate = jax.nn.sigmoid(bias(2) + lora(lerp(xc, xd, mu(6)), v1_ref, v2_ref))
    v = v + (v_first - v) * gate
  ag = jax.nn.sigmoid(bias(1) + lora(lerp(xc, xd, mu(4)), a1_ref, a2_ref))
  g = lora(lerp(xc, xd, mu(5)), g1_ref, g2_ref, jax.nn.sigmoid)
  return r, w, k, v, ag, g


def _rwkv_prep_sample_kernel(*refs, has_vfirst):
  cur_refs, prev_refs = refs[0:4], refs[4:8]
  mu_ref, bias_ref = refs[8:10]
  lora_refs = refs[10:18]
  pos = 18
  vf = None
  if has_vfirst:
    vf = refs[pos][...]
    pos += 1
  out_refs = refs[pos:pos + 6]
  cur = [c[...] for c in cur_refs]
  delta = [p[...] - c for p, c in zip(prev_refs, cur)]
  outs = _rwkv_math(cur, delta, mu_ref, bias_ref, lora_refs, vf)
  for o_ref, val in zip(out_refs, outs):
    o_ref[...] = val


def _rwkv_prep_sample(p, z, prev, mu7, bias3, loras, v_first, d_b, col0):
  has_vfirst = v_first is not None
  ins = [z] * 4 + [prev] * 4 + [mu7, bias3] + list(loras)
  specs = [p.row_spec(d_b, col0 + s) for s in range(4)] + [p.row_spec(d_b, s) for s in range(4)]
  specs += [_full_spec(mu7.shape), _full_spec(bias3.shape)] + [_full_spec(m.shape) for m in loras]
  if has_vfirst:
    ins.append(v_first)
    specs.append(p.row_spec(d_b))
  tile = _nbytes((p.tm, d_b), F32)
  out = jax.ShapeDtypeStruct((p.m, d_b), F32)
  return pl.pallas_call(
      functools.partial(_rwkv_prep_sample_kernel, has_vfirst=has_vfirst),
      out_shape=(out,) * 6,
      grid=(p.n_tiles,),
      in_specs=specs,
      out_specs=(p.row_spec(d_b),) * 6,
      compiler_params=_row_params(15 * tile + 8 * _nbytes(loras[0].shape, BF16), 10 * tile),
      name=f"rwkv_prep_{p.name}",
  )(*ins)


def _lane_parts(arrs, h_b):
  lane = lax.broadcasted_iota(jnp.int32, arrs[0].shape, 1)
  out = arrs[0]
  for k in range(1, len(arrs)):
    out = jnp.where(lane >= k * h_b, arrs[k], out)
  return out


def _interleave(vals, h_b):
  nb = len(vals)
  width = nb * h_b
  tiles = []
  for c in range(HEAD_B // nb):
    src = [v[:, c * width:(c + 1) * width] for v in vals]
    rolled = []
    for r in range(nb):
      merged = _lane_parts([src[(q + r) % nb] for q in range(nb)], h_b)
      rolled.append(merged if r == 0 else pltpu.roll(merged, r * h_b, axis=1))
    for q in range(nb):
      tiles.append(_lane_parts([rolled[(b - q) % nb] for b in range(nb)], h_b))
  return jnp.concatenate(tiles, axis=1)


def _deinterleave(row_ref, nb, h_b):
  width = nb * h_b
  cols = [[] for _ in range(nb)]
  for c in range(HEAD_B // nb):
    tiles = [row_ref[:, (nb * c + q) * width:(nb * c + q + 1) * width] for q in range(nb)]
    rolled = []
    for r in range(nb):
      merged = _lane_parts([tiles[(part + r) % nb] for part in range(nb)], h_b)
      rolled.append(merged if r == 0 else pltpu.roll(merged, r * h_b, axis=1))
    for b in range(nb):
      cols[b].append(_lane_parts([rolled[(q - b) % nb] for q in range(nb)], h_b))
  return [jnp.concatenate(cb, axis=1) for cb in cols]


def _rwkv_prep_prompt_kernel(*refs, nb, tm, h_b, has_vfirst, emit_v):
  cur_refs, shift0_ref = refs[0:4], refs[4]
  mu_ref, bias_ref = refs[5:7]
  lora_refs = refs[7:15]
  pos = 15
  vf = None
  rows = nb * tm
  if has_vfirst:
    vf = refs[pos][...].reshape(rows, -1)
    pos += 1
  row_out_refs = refs[pos:pos + 5]
  g_ref = refs[pos + 5]
  pos += 6
  if emit_v:
    v_tok_ref = refs[pos]
    pos += 1
  carry_ref = refs[pos]

  @pl.when(pl.program_id(0) == 0)
  def _():
    d_b = carry_ref.shape[2]
    for s in range(4):
      carry_ref[s] = shift0_ref[:, s * d_b:(s + 1) * d_b]

  cur, delta = [], []
  for s, c_ref in enumerate(cur_refs):
    c = c_ref[...].reshape(rows, -1)
    rolled = pltpu.roll(c, 1, axis=0)
    sub = V7X_SUBLANES
    first = lax.broadcasted_iota(jnp.int32, (sub, c.shape[1]), 0) == 0
    pieces = []
    for b in range(nb):
      head = jnp.where(first, carry_ref[s, b:b + 1, :], rolled[b * tm:b * tm + sub])
      pieces += [head, rolled[b * tm + sub:(b + 1) * tm]]
    prev = jnp.concatenate(pieces, axis=0)
    for b in range(nb):
      carry_ref[s, b:b + 1, :] = c_ref[b, tm - 1:tm, :]
    cur.append(c)
    delta.append(prev - c)
  r, w, k, v, ag, g = _rwkv_math(cur, delta, mu_ref, bias_ref, lora_refs, vf)
  for o_ref, val in zip(row_out_refs, (r, w, k, v, ag)):
    o_ref[...] = _interleave([val[b * tm:(b + 1) * tm] for b in range(nb)], h_b)
  g_ref[...] = g.reshape(nb, tm, -1)
  if emit_v:
    v_tok_ref[...] = v.reshape(nb, tm, -1)


def _rwkv_prep_prompt(p, zb, shift0, mu7, bias3, loras, v_first, d_b, emit_v):
  nb, t = p.b, p.t
  h_b = d_b // HEAD_B
  tm = _tile(t, 32, V7X_SUBLANES)
  has_vfirst = v_first is not None
  tok_spec = lambda c=0: pl.BlockSpec((nb, tm, d_b), lambda i, c=c: (0, i, c))
  ins = [zb] * 4 + [shift0, mu7, bias3] + list(loras)
  specs = [tok_spec(s) for s in range(4)]
  specs += [_full_spec(shift0.shape), _full_spec(mu7.shape), _full_spec(bias3.shape)]
  specs += [_full_spec(m.shape) for m in loras]
  if has_vfirst:
    ins.append(v_first)
    specs.append(tok_spec())
  width = HEAD_B * nb * h_b
  row_shape = jax.ShapeDtypeStruct((t, width), F32)
  tok_shape = jax.ShapeDtypeStruct((nb, t, d_b), F32)
  out_shape = (row_shape,) * 5 + (tok_shape,) + ((tok_shape,) if emit_v else ())
  out_specs = ((pl.BlockSpec((tm, width), lambda i: (i, 0)),) * 5 + (tok_spec(),)
               + ((tok_spec(),) if emit_v else ()))
  tile = _nbytes((nb * tm, d_b), F32)
  return pl.pallas_call(
      functools.partial(_rwkv_prep_prompt_kernel, nb=nb, tm=tm, h_b=h_b, has_vfirst=has_vfirst,
                        emit_v=emit_v),
      out_shape=out_shape,
      grid=(t // tm,),
      in_specs=specs,
      out_specs=out_specs,
      scratch_shapes=[pltpu.VMEM((4, nb, d_b), F32)],
      compiler_params=pltpu.CompilerParams(
          dimension_semantics=("arbitrary",),
          vmem_limit_bytes=_vmem_limit(12 * tile + 8 * _nbytes(loras[0].shape, BF16), 0,
                                       16 * tile)),
      name=f"rwkv_prep_{p.name}",
  )(*ins)


def _wkv_kernel(r_ref, w_ref, k_ref, v_ref, ag_ref, kk_ref, ka_ref, rk_ref, lnw_ref, lnb_ref,
                s0_ref, o_ref, s_ref, a_scr, b_scr, km_scr, yr_scr, vt_scr, yt_scr, *, tb, n, l):
  @pl.when(pl.program_id(1) == 0)
  def _():
    s_ref[...] = s0_ref[...]

  def tile_sum(x):
    acc = x[:, 0:l]
    for j in range(1, n):
      acc = acc + x[:, j * l:(j + 1) * l]
    return acc

  rep = lambda x: jnp.concatenate([x] * n, axis=1)
  row = lambda ref, u, j: ref[u:u + 1, j * l:(j + 1) * l]

  k_raw = k_ref[...]
  ag = ag_ref[...]
  kk = k_raw * kk_ref[...]
  kk = kk / rep(jnp.maximum(jnp.sqrt(tile_sum(kk * kk)), 1e-12))
  a_scr[0:tb, :] = -kk
  a_scr[tb:tb + V7X_SUBLANES, :] = jnp.zeros((V7X_SUBLANES, n * l), F32)
  b_scr[...] = kk * ag
  km_scr[...] = k_raw * (1.0 + (ag - 1.0) * ka_ref[...])

  sa0 = jnp.zeros((n, l), F32)
  for j in range(n):
    sa0 = sa0 + s_ref[j] * row(a_scr, 0, j)

  group = V7X_SUBLANES if tb % V7X_SUBLANES == 0 else tb

  def step_group(gi, sa):
    if tb == group:
      win = lambda ref, off=0: ref
      a_next = None
    else:
      base = pl.multiple_of(gi * group, group)
      win = lambda ref, off=0: ref.at[pl.ds(pl.multiple_of(base + off, group), group), :]
      a_next = win(a_scr, group)
    w_w, b_w, k_w, r_w, v_w, a_w, y_w = (win(w_ref), win(b_scr), win(km_scr), win(r_ref),
                                         win(v_ref), win(a_scr), win(yr_scr))
    for u in range(group):
      for i in range(n):
        vt_scr[i:i + 1, :] = row(v_w, u, i)
      v_t = vt_scr[...]
      y = jnp.zeros_like(sa)
      sa_next = jnp.zeros_like(sa)
      for j in range(n):
        s_new = s_ref[j] * row(w_w, u, j) + sa * row(b_w, u, j) + v_t * row(k_w, u, j)
        s_ref[j] = s_new
        y = y + s_new * row(r_w, u, j)
        if u + 1 < group or a_next is None:
          sa_next = sa_next + s_new * row(a_w, u + 1, j)
        else:
          sa_next = sa_next + s_new * row(a_next, 0, j)
      yt_scr[...] = y
      for i in range(n):
        y_w[u:u + 1, i * l:(i + 1) * l] = yt_scr[i:i + 1, :]
      sa = sa_next
    return sa

  if tb == group:
    step_group(0, sa0)
  else:
    lax.fori_loop(0, tb // group, step_group, sa0)

  y = yr_scr[...]
  yc = y - rep(tile_sum(y) * (1.0 / n))
  var = tile_sum(yc * yc) * (1.0 / n)
  yn = yc * rep(lax.rsqrt(var + EPS_GN)) * lnw_ref[...] + lnb_ref[...]
  rk = tile_sum(r_ref[...] * km_scr[...] * rk_ref[...])
  o_ref[...] = yn + rep(rk) * v_ref[...]


def _wkv(seq, params, s0, *, tb, name):
  g, t, width = seq[0].shape
  n = HEAD_B
  l = width // n
  tb = _tile(t, tb, V7X_SUBLANES)
  seq_spec = pl.BlockSpec((None, tb, width), lambda gi, ti: (gi, ti, 0))
  par_spec = pl.BlockSpec((None, 1, width), lambda gi, ti: (gi, 0, 0))
  st_spec = pl.BlockSpec((None, n, n, l), lambda gi, ti: (gi, 0, 0, 0))
  blk = _nbytes((tb, width), F32)
  st = _nbytes((n, n, l), F32)
  return pl.pallas_call(
      functools.partial(_wkv_kernel, tb=tb, n=n, l=l),
      out_shape=(jax.ShapeDtypeStruct((g, t, width), F32), jax.ShapeDtypeStruct((g, n, n, l), F32)),
      grid=(g, t // tb),
      in_specs=[seq_spec] * 5 + [par_spec] * 5 + [st_spec],
      out_specs=(seq_spec, st_spec),
      scratch_shapes=[pltpu.VMEM((tb + V7X_SUBLANES, width), F32), pltpu.VMEM((tb, width), F32),
                      pltpu.VMEM((tb, width), F32), pltpu.VMEM((tb, width), F32),
                      pltpu.VMEM((n, l), F32), pltpu.VMEM((n, l), F32)],
      compiler_params=pltpu.CompilerParams(
          dimension_semantics=("parallel", "arbitrary"),
          vmem_limit_bytes=_vmem_limit(6 * blk + 2 * st + 5 * _nbytes((1, width), F32),
                                       6 * blk, 10 * blk)),
      name=name,
  )(*seq, *params, s0)


def _cat_sample_kernel(oa_ref, ob_ref, g_ref, o_ref, *, d_a):
  o_ref[:, :d_a] = oa_ref[...]
  o_ref[:, d_a:] = (ob_ref[...] * g_ref[...]).astype(o_ref.dtype)


def _cat_sample(p, oa, ob, g, d_a, d_b):
  d = d_a + d_b
  blocks = _nbytes((p.tm, d_a), BF16) + 2 * _nbytes((p.tm, d_b), F32) + _nbytes((p.tm, d), BF16)
  return pl.pallas_call(
      functools.partial(_cat_sample_kernel, d_a=d_a),
      out_shape=jax.ShapeDtypeStruct((p.m, d), BF16),
      grid=(p.n_tiles,),
      in_specs=[p.row_spec(d_a), p.row_spec(d_b), p.row_spec(d_b)],
      out_specs=p.row_spec(d),
      compiler_params=_row_params(blocks, _nbytes((p.tm, d_b), F32)),
      name=f"cat_{p.name}",
  )(oa, ob, g)


def _cat_prompt_kernel(oa_ref, ob_ref, g_ref, o_ref, *, nb, h_b, d_a):
  o_ref[:, :, :d_a] = oa_ref[...]
  for b, ob in enumerate(_deinterleave(ob_ref, nb, h_b)):
    o_ref[b, :, d_a:] = (ob * g_ref[b]).astype(o_ref.dtype)


def _cat_prompt(p, oa, ob, g, d_a, d_b):
  nb, t = p.b, p.t
  h_b = d_b // HEAD_B
  d = d_a + d_b
  tm = _tile(t, 64, V7X_SUBLANES)
  width = ob.shape[1]
  blocks = (_nbytes((nb, tm, d_a), BF16) + _nbytes((tm, width), F32) + _nbytes((nb, tm, d_b), F32)
            + _nbytes((nb, tm, d), BF16))
  return pl.pallas_call(
      functools.partial(_cat_prompt_kernel, nb=nb, h_b=h_b, d_a=d_a),
      out_shape=jax.ShapeDtypeStruct((nb, t, d), BF16),
      grid=(t // tm,),
      in_specs=[pl.BlockSpec((nb, tm, d_a), lambda i: (0, i, 0)),
                pl.BlockSpec((tm, width), lambda i: (i, 0)),
                pl.BlockSpec((nb, tm, d_b), lambda i: (0, i, 0))],
      out_specs=pl.BlockSpec((nb, tm, d), lambda i: (0, i, 0)),
      compiler_params=_row_params(blocks, 3 * _nbytes((tm, width), F32)),
      name=f"cat_{p.name}",
  )(oa, ob, g)


def _to_jh(x, h_b):
  s = x.shape[:-1]
  return x.reshape(s + (h_b, HEAD_B)).swapaxes(-1, -2).reshape(s + (h_b * HEAD_B,))


def _from_jh(x, h_b):
  s = x.shape[:-1]
  return x.reshape(s + (HEAD_B, h_b)).swapaxes(-1, -2).reshape(s + (h_b * HEAD_B,))


def _slabs_to_jh(x, h_b):
  s = x.shape[:-1]
  return _to_jh(x.reshape(s + (4, h_b * HEAD_B)), h_b).reshape(s + (4 * h_b * HEAD_B,))


def _slabs_from_jh(x, h_b):
  s = x.shape[:-1]
  return _from_jh(x.reshape(s + (4, h_b * HEAD_B)), h_b).reshape(s + (4 * h_b * HEAD_B,))


def _scan_param(p, v, h_b):
  vh = v.reshape(h_b, HEAD_B)
  if p.time_major:
    return jnp.repeat(vh, p.b, axis=1).reshape(h_b, 1, HEAD_B * p.b)
  return jnp.tile(vh.T, (1, p.b)).reshape(1, 1, HEAD_B * p.b * h_b)


def _state_to_scan(p, s):
  if p.time_major:
    return s.transpose(1, 3, 2, 0)
  b, h_b = s.shape[:2]
  return s.transpose(3, 2, 0, 1).reshape(1, HEAD_B, HEAD_B, b * h_b)


def _state_from_scan(p, s, h_b):
  if p.time_major:
    return s.transpose(3, 0, 2, 1)
  return s.reshape(HEAD_B, HEAD_B, p.b, h_b).transpose(2, 3, 1, 0)


def _trunk(p, x, mod_l, shift0, wkv0, weights, keep_chunk_rows):
  (g_pre_mix, g_post_mix, g_pre_ffn, g_post_ffn, w_in, w_in_b, w_out_a, w_out_b, ln_v_w, ln_v_b,
   w_spatial, b_spatial, mu7s, bias3s, loras, k_k, k_a, r_k, ln_x_w, ln_x_b, w_up, w_down) = weights
  depth = w_in.shape[0]
  d = p.d
  d_a = d // 2
  d_b = d - d_a
  h_b = d_b // HEAD_B
  mm_tm = 1024 if not p.time_major else 512
  new_wkv, new_shift, chunk_rows = [], [], []
  v_first = None
  h = _prenorm(p, x, mod_l[0], g_pre_mix[0], 1, 0)
  for l in range(depth):
    mod = mod_l[l]
    za = _matmul_f32w(h, w_in, l, 2 * d_a, F32, tm=mm_tm, name=f"mm_in_a_{p.name}")
    zb = _matmul(h, w_in_b, l, F32, tm=mm_tm, name=f"mm_in_b_{p.name}")
    par = [_scan_param(p, t, h_b) for t in (k_k[l], k_a[l], r_k[l].reshape(-1), ln_x_w[l], ln_x_b[l])]
    s0 = _state_to_scan(p, wkv0[l])
    if p.time_major:
      out_a, va_n = _sgu_sample(p, za, ln_v_w[l], ln_v_b[l], w_spatial[l], b_spatial[l], d_a)
      if keep_chunk_rows:
        chunk_rows.append(va_n)
      prev = jnp.concatenate([shift0[l], zb[:-p.b]], axis=0)
      new_shift.append(zb[(p.t - 1) * p.b:])
      r, w, k, v, ag, g = _rwkv_prep_sample(p, zb, prev, mu7s[l], bias3s[l], loras[l], v_first,
                                            d_b, 0)
      if l == 0:
        v_first = v
      to_scan = lambda a: (a.reshape(p.t, p.b, HEAD_B, h_b).transpose(3, 0, 2, 1)
                           .reshape(h_b, p.t, HEAD_B * p.b))
      ob, s_new = _wkv([to_scan(a) for a in (r, w, k, v, ag)], par, s0, tb=32,
                       name=f"wkv_{p.name}")
      ob = ob.reshape(h_b, p.t, HEAD_B, p.b).transpose(1, 3, 2, 0).reshape(p.m, d_b)
      cat = _cat_sample(p, out_a, ob, g, d_a, d_b)
    else:
      out_a = _sgu_prompt(p, za, ln_v_w[l], ln_v_b[l], w_spatial[l], b_spatial[l], d_a)
      zb3 = zb.reshape(p.b, p.t, 4 * d_b)
      new_shift.append(zb3[:, p.t - 1])
      outs = _rwkv_prep_prompt(p, zb3, shift0[l], mu7s[l], bias3s[l], loras[l], v_first, d_b,
                               emit_v=(l == 0))
      if l == 0:
        v_first = outs[6]
      seq = [a[None] for a in outs[:5]]
      ob, s_new = _wkv(seq, par, s0, tb=32, name=f"wkv_{p.name}")
      cat = _cat_prompt(p, out_a.reshape(p.b, p.t, d_a), ob[0], outs[5], d_a, d_b)
      cat = cat.reshape(p.m, d)
    new_wkv.append(_state_from_scan(p, s_new, h_b))
    y = _matmul_pair(cat, d_a, w_out_a, w_out_b, l, F32, tm=mm_tm, name=f"mm_out_{p.name}")
    x, h = _resid_prenorm(p, x, y, mod, 2, g_post_mix[l], mod, g_pre_ffn[l], 4, 3)
    f = _matmul(h, w_up, l, BF16, act=_relu2, tm=mm_tm, name=f"mm_up_{p.name}")
    f = _matmul(f, w_down, l, F32, tm=mm_tm, name=f"mm_down_{p.name}")
    if l + 1 < depth:
      x, h = _resid_prenorm(p, x, f, mod, 5, g_post_ffn[l], mod_l[l + 1], g_pre_mix[l + 1], 1, 0)
    else:
      x = _resid(p, x, f, mod, 5, g_post_ffn[l])
  rows = jnp.stack(chunk_rows) if keep_chunk_rows else None
  return x, jnp.stack(new_wkv), jnp.stack(new_shift), rows


def _pad_lora(m1, m2, h_b):
  rank = m1.shape[1]
  rp = -(-rank // V7X_LANES) * V7X_LANES
  m1 = jnp.pad(_to_jh(m1.T, h_b).T, ((0, 0), (0, rp - rank))).astype(BF16)
  m2 = jnp.pad(_to_jh(m2, h_b), ((0, rp - rank), (0, 0))).astype(BF16)
  return m1, m2


def kernel(x_prompt, x_sample, state_wkv, state_shift, c_prompt, c_sample, w_ada, b_ada, g_pre_mix, g_post_mix, g_pre_ffn, g_post_ffn, w_in, w_out, ln_v_w, ln_v_b, w_spatial, b_spatial, mu_rkv, mu_lora, mu_vm, w0, w1, w2, a0, a1, a2, v0, v1, v2, g1, g2, k_k, k_a, r_k, ln_x_w, ln_x_b, w_up, w_down):
  bp, tp, d = x_prompt.shape
  bs, ts, _ = x_sample.shape
  depth = w_in.shape[0]
  d_a = d // 2
  d_b = d - d_a
  h_b = d_b // HEAD_B
  p_a = 2 * d_a

  n_c = bs + bp
  c_all = jnp.concatenate([c_sample, c_prompt, jnp.zeros((-n_c % V7X_SUBLANES, d), F32)], axis=0)
  mod_all = _ada_all(c_all, w_ada, b_ada)
  mod_s = [mod_all[l, :bs] for l in range(depth)]
  mod_p = [mod_all[l, bs:n_c].reshape(bp, 1, N_MOD * d) for l in range(depth)]

  w_in_b = _slabs_to_jh(w_in[..., p_a:], h_b).astype(BF16)
  w_out_a = w_out[:, :d_a].astype(BF16)
  w_out_b = (w_out[:, d_a:].reshape(depth, h_b, HEAD_B, d).swapaxes(1, 2).reshape(depth, d_b, d)
             .astype(BF16))
  loras, mu7s, bias3s = [], [], []
  for l in range(depth):
    lv = max(l - 1, 0)
    loras.append(_pad_lora(w1[l], w2[l], h_b) + _pad_lora(a1[l], a2[l], h_b)
                 + _pad_lora(g1[l], g2[l], h_b) + _pad_lora(v1[lv], v2[lv], h_b))
    mu7s.append(_to_jh(jnp.concatenate([mu_rkv[l], mu_lora[l], mu_vm[lv][None]], axis=0), h_b))
    bias3s.append(_to_jh(jnp.stack([w0[l], a0[l], v0[lv]]), h_b))
  weights = (g_pre_mix, g_post_mix, g_pre_ffn, g_post_ffn, w_in, w_in_b, w_out_a, w_out_b,
             ln_v_w, ln_v_b, w_spatial, b_spatial, mu7s, bias3s, loras, k_k, k_a, r_k, ln_x_w,
             ln_x_b, w_up.astype(BF16), w_down.astype(BF16))

  pp = _Pass("prompt", bp, tp, d, time_major=False)
  shift0 = jnp.zeros((depth, bp, 4 * d_b), x_prompt.dtype)
  wkv0 = jnp.zeros((depth, bp, h_b, HEAD_B, HEAD_B), state_wkv.dtype)
  y_p, wkv_p, shift_p, _ = _trunk(pp, x_prompt.reshape(bp * tp, d), mod_p, shift0, wkv0,
                                  weights, False)

  ps = _Pass("sample", bs, ts, d, time_major=True)
  x_s = x_sample.transpose(1, 0, 2).reshape(ts * bs, d)
  y_s, wkv_s, shift_s, rows_s = _trunk(ps, x_s, mod_s, _slabs_to_jh(state_shift, h_b), state_wkv,
                                       weights, True)

  y_sample = y_s.reshape(ts, bs, d).transpose(1, 0, 2)
  chunk_v = rows_s.reshape(depth, ts, bs, d_a).transpose(0, 2, 1, 3)
  return (y_p.reshape(bp, tp, d), y_sample, wkv_p, _slabs_from_jh(shift_p, h_b), wkv_s,
          _slabs_from_jh(shift_s, h_b), chunk_v)
```

```python
import functools
import math

import jax
import jax.numpy as jnp
from jax import lax
from jax.experimental import pallas as pl
from jax.experimental.pallas import tpu as pltpu

F32 = jnp.float32
BF16 = jnp.bfloat16

H_A = 8
CHUNK = 128
HEAD_B = 64
N_MOD = 6
EPS_RMS = 1e-6
EPS_LN = 1e-5
EPS_GN = 64e-5

V7X_LANES = 128
V7X_SUBLANES = 8
VMEM_LIMIT_CAP = 56 * 1024 * 1024

MM_TM, MM_TN, MM_TK = 1024, 1024, 4096
MM_TM_SAMPLE = 512
ADA_TK, ADA_TN = 1024, 2048
CAST_TK, CAST_TN = 512, 2048
PREP_TM = 32
CAT_TM = 64
WKV_TB = 32


def _vmem_limit(block_bytes, scratch_bytes=0, temp_bytes=0):
  est = 2 * block_bytes + scratch_bytes + temp_bytes + (2 << 20)
  return int(min(max(est, 16 << 20), VMEM_LIMIT_CAP))


def _tile(dim, target, align=V7X_LANES):
  if dim <= target:
    return dim
  t = (target // align) * align
  while t >= align:
    if dim % t == 0:
      return t
    t -= align
  return dim


def _nbytes(shape, dtype):
  n = 1
  for s in shape:
    n *= s
  return n * jnp.dtype(dtype).itemsize


def _relu2(x):
  r = jnp.maximum(x, 0.0)
  return r * r


def _mm_kernel(a_ref, b_ref, o_ref, *scratch, nk, act):
  if nk == 1:
    acc = jnp.dot(a_ref[...], b_ref[...], preferred_element_type=F32)
    o_ref[...] = (act(acc) if act else acc).astype(o_ref.dtype)
    return
  k = pl.program_id(2)
  if not scratch:
    @pl.when(k == 0)
    def _():
      o_ref[...] = jnp.dot(a_ref[...], b_ref[...], preferred_element_type=F32)

    @pl.when(k > 0)
    def _():
      o_ref[...] += jnp.dot(a_ref[...], b_ref[...], preferred_element_type=F32)
    return
  (acc_ref,) = scratch

  @pl.when(k == 0)
  def _():
    acc_ref[...] = jnp.zeros_like(acc_ref)

  acc_ref[...] += jnp.dot(a_ref[...], b_ref[...], preferred_element_type=F32)

  @pl.when(k == nk - 1)
  def _():
    acc = acc_ref[...]
    o_ref[...] = (act(acc) if act else acc).astype(o_ref.dtype)


def _matmul(a, w, layer, out_dtype, *, act=None, tm=MM_TM, tn=MM_TN, tk=MM_TK, name="matmul"):
  m, kdim = a.shape
  n = w.shape[2]
  tm, tn, tk = _tile(m, tm, V7X_SUBLANES), _tile(n, tn), _tile(kdim, tk)
  nk = kdim // tk
  blocks = (_nbytes((tm, tk), a.dtype) + _nbytes((tk, tn), w.dtype)
            + _nbytes((tm, tn), out_dtype))
  acc_bytes = _nbytes((tm, tn), F32)
  need_acc = nk > 1 and (act is not None or jnp.dtype(out_dtype) != jnp.dtype(F32))
  return pl.pallas_call(
      functools.partial(_mm_kernel, nk=nk, act=act),
      out_shape=jax.ShapeDtypeStruct((m, n), out_dtype),
      grid=(m // tm, n // tn, nk),
      in_specs=[pl.BlockSpec((tm, tk), lambda i, j, k: (i, k)),
                pl.BlockSpec((None, tk, tn), lambda i, j, k: (layer, k, j))],
      out_specs=pl.BlockSpec((tm, tn), lambda i, j, k: (i, j)),
      scratch_shapes=[pltpu.VMEM((tm, tn), F32)] if need_acc else [],
      compiler_params=pltpu.CompilerParams(
          dimension_semantics=("parallel", "parallel", "arbitrary"),
          vmem_limit_bytes=_vmem_limit(blocks, acc_bytes if need_acc else 0, 2 * acc_bytes)),
      name=name,
  )(a, w)


def _cast_kernel(w_ref, o_ref):
  o_ref[...] = w_ref[...].astype(o_ref.dtype)


def _cast_cols(w, col_start, n_cols, name):
  depth, kdim, _ = w.shape
  tk, tn = _tile(kdim, CAST_TK, V7X_SUBLANES), _tile(math.gcd(col_start, n_cols), CAST_TN)
  off = col_start // tn
  return pl.pallas_call(
      _cast_kernel,
      out_shape=jax.ShapeDtypeStruct((depth, kdim, n_cols), BF16),
      grid=(depth, kdim // tk, n_cols // tn),
      in_specs=[pl.BlockSpec((None, tk, tn), lambda l, k, j: (l, k, j + off))],
      out_specs=pl.BlockSpec((None, tk, tn), lambda l, k, j: (l, k, j)),
      compiler_params=pltpu.CompilerParams(
          dimension_semantics=("parallel", "parallel", "parallel"),
          vmem_limit_bytes=_vmem_limit(_nbytes((tk, tn), F32) + _nbytes((tk, tn), BF16))),
      name=name,
  )(w)


def _mm_pair_kernel(a1_ref, a2_ref, b1_ref, b2_ref, o_ref):
  acc = jnp.dot(a1_ref[...], b1_ref[...], preferred_element_type=F32)
  acc = acc + jnp.dot(a2_ref[...], b2_ref[...], preferred_element_type=F32)
  o_ref[...] = acc.astype(o_ref.dtype)


def _matmul_pair(a, k1, w1, w2, layer, out_dtype, *, tm=MM_TM, tn=MM_TN, name="matmul"):
  m, kdim = a.shape
  k2 = kdim - k1
  assert k1 == k2 and w1.shape[1] == k1 and w2.shape[1] == k2
  n = w1.shape[2]
  tm, tn = _tile(m, tm, V7X_SUBLANES), _tile(n, tn)
  blocks = (_nbytes((tm, kdim), a.dtype) + _nbytes((kdim, tn), w1.dtype)
            + _nbytes((tm, tn), out_dtype))
  return pl.pallas_call(
      _mm_pair_kernel,
      out_shape=jax.ShapeDtypeStruct((m, n), out_dtype),
      grid=(m // tm, n // tn),
      in_specs=[pl.BlockSpec((tm, k1), lambda i, j: (i, 0)),
                pl.BlockSpec((tm, k2), lambda i, j: (i, 1)),
                pl.BlockSpec((None, k1, tn), lambda i, j: (layer, 0, j)),
                pl.BlockSpec((None, k2, tn), lambda i, j: (layer, 0, j))],
      out_specs=pl.BlockSpec((tm, tn), lambda i, j: (i, j)),
      compiler_params=pltpu.CompilerParams(
          dimension_semantics=("parallel", "parallel"),
          vmem_limit_bytes=_vmem_limit(blocks, 0, 2 * _nbytes((tm, tn), F32))),
      name=name,
  )(a, a, w1, w2)


def _ada_kernel(c_ref, w_ref, b_ref, o_ref, acc_ref, *, nk):
  k = pl.program_id(2)

  @pl.when(k == 0)
  def _():
    acc_ref[...] = jnp.zeros_like(acc_ref)

  c = c_ref[...]
  c_act = (c * jax.nn.sigmoid(c)).astype(BF16)
  acc_ref[...] += jnp.dot(c_act, w_ref[...].astype(BF16), preferred_element_type=F32)

  @pl.when(k == nk - 1)
  def _():
    o_ref[...] = acc_ref[...] + b_ref[...]


def _ada_all(c_all, w_ada, b_ada):
  depth, d, n = w_ada.shape
  mp = c_all.shape[0]
  tk, tn = _tile(d, ADA_TK), _tile(n, ADA_TN)
  nk = d // tk
  blocks = (_nbytes((mp, tk), F32) + _nbytes((tk, tn), F32) + _nbytes((1, tn), F32)
            + _nbytes((mp, tn), F32))
  return pl.pallas_call(
      functools.partial(_ada_kernel, nk=nk),
      out_shape=jax.ShapeDtypeStruct((depth, mp, n), F32),
      grid=(depth, n // tn, nk),
      in_specs=[pl.BlockSpec((mp, tk), lambda l, j, k: (0, k)),
                pl.BlockSpec((None, tk, tn), lambda l, j, k: (l, k, j)),
                pl.BlockSpec((None, 1, tn), lambda l, j, k: (l, 0, j))],
      out_specs=pl.BlockSpec((None, mp, tn), lambda l, j, k: (l, 0, j)),
      scratch_shapes=[pltpu.VMEM((mp, tn), F32)],
      compiler_params=pltpu.CompilerParams(
          dimension_semantics=("parallel", "parallel", "arbitrary"),
          vmem_limit_bytes=_vmem_limit(blocks, _nbytes((mp, tn), F32),
                                       _nbytes((tk, tn), BF16))),
      name="ada_mod",
  )(c_all, w_ada, b_ada.reshape(depth, 1, n))


class _Pass:
  def __init__(self, name, b, t, d, time_major):
    self.name, self.b, self.t, self.d, self.time_major = name, b, t, d, time_major
    self.m = b * t
    self.tm = b if time_major else _tile(t, CHUNK, V7X_SUBLANES)
    self.n_tiles = self.m // self.tm
    self.tiles_per_seq = 1 if time_major else t // self.tm

  def row_spec(self, width, col=0):
    return pl.BlockSpec((self.tm, width), lambda i, c=col: (i, c))

  def mod_spec(self, which):
    d = self.d
    if self.time_major:
      return pl.BlockSpec((self.tm, d), lambda i, c=which: (0, c))
    tps = self.tiles_per_seq
    return pl.BlockSpec((None, 1, d), lambda i, c=which: (i // tps, 0, c))

  def mod_rows(self):
    return self.tm if self.time_major else 1


def _vec_spec(width):
  return pl.BlockSpec((1, width), lambda i: (0, 0))


def _full_spec(shape):
  nd = len(shape)
  return pl.BlockSpec(shape, lambda i, nd=nd: (0,) * nd)


def _row_params(blocks, temps=0):
  return pltpu.CompilerParams(
      dimension_semantics=("parallel",),
      vmem_limit_bytes=_vmem_limit(blocks, 0, temps))


def _rms(x, g):
  ms = jnp.mean(x * x, axis=-1, keepdims=True)
  return x * lax.rsqrt(ms + EPS_RMS) * g


def _prenorm_kernel(x_ref, g_ref, sc_ref, sh_ref, h_ref):
  h = _rms(x_ref[...], g_ref[...]) * (1.0 + sc_ref[...]) + sh_ref[...]
  h_ref[...] = h.astype(h_ref.dtype)


def _resid_kernel(x_ref, y_ref, gt_ref, gpost_ref, xo_ref):
  xo_ref[...] = x_ref[...] + gt_ref[...] * _rms(y_ref[...], gpost_ref[...])


def _resid_prenorm_kernel(x_ref, y_ref, gt_ref, gpost_ref, gpre_ref, sc_ref, sh_ref,
                          xo_ref, h_ref):
  xo = x_ref[...] + gt_ref[...] * _rms(y_ref[...], gpost_ref[...])
  xo_ref[...] = xo
  h = _rms(xo, gpre_ref[...]) * (1.0 + sc_ref[...]) + sh_ref[...]
  h_ref[...] = h.astype(h_ref.dtype)


def _prenorm(p, x, mod, g_pre, sc_idx, sh_idx):
  d = p.d
  blocks = _nbytes((p.tm, d), F32) + _nbytes((p.tm, d), BF16) + 3 * _nbytes((p.mod_rows(), d), F32)
  return pl.pallas_call(
      _prenorm_kernel,
      out_shape=jax.ShapeDtypeStruct((p.m, d), BF16),
      grid=(p.n_tiles,),
      in_specs=[p.row_spec(d), _vec_spec(d), p.mod_spec(sc_idx), p.mod_spec(sh_idx)],
      out_specs=p.row_spec(d),
      compiler_params=_row_params(blocks, 3 * _nbytes((p.tm, d), F32)),
      name=f"prenorm_{p.name}",
  )(x, g_pre.reshape(1, d), mod, mod)


def _resid(p, x, y, mod, gt_idx, g_post):
  d = p.d
  blocks = 3 * _nbytes((p.tm, d), F32) + 2 * _nbytes((p.mod_rows(), d), F32)
  return pl.pallas_call(
      _resid_kernel,
      out_shape=jax.ShapeDtypeStruct((p.m, d), F32),
      grid=(p.n_tiles,),
      in_specs=[p.row_spec(d), p.row_spec(d), p.mod_spec(gt_idx), _vec_spec(d)],
      out_specs=p.row_spec(d),
      compiler_params=_row_params(blocks, 3 * _nbytes((p.tm, d), F32)),
      name=f"resid_{p.name}",
  )(x, y, mod, g_post.reshape(1, d))


def _resid_prenorm(p, x, y, mod, gt_idx, g_post, mod_pre, g_pre, sc_idx, sh_idx):
  d = p.d
  blocks = (3 * _nbytes((p.tm, d), F32) + _nbytes((p.tm, d), BF16)
            + 5 * _nbytes((p.mod_rows(), d), F32))
  return pl.pallas_call(
      _resid_prenorm_kernel,
      out_shape=(jax.ShapeDtypeStruct((p.m, d), F32), jax.ShapeDtypeStruct((p.m, d), BF16)),
      grid=(p.n_tiles,),
      in_specs=[p.row_spec(d), p.row_spec(d), p.mod_spec(gt_idx), _vec_spec(d), _vec_spec(d),
                p.mod_spec(sc_idx), p.mod_spec(sh_idx)],
      out_specs=(p.row_spec(d), p.row_spec(d)),
      compiler_params=_row_params(blocks, 4 * _nbytes((p.tm, d), F32)),
      name=f"resid_prenorm_{p.name}",
  )(x, y, mod, g_post.reshape(1, d), g_pre.reshape(1, d), mod_pre, mod_pre)


def _gelu_exact(x):
  return 0.5 * x * (1.0 + lax.erf(x * (0.5 ** 0.5)))


def _layer_norm(x, w, b):
  mu = jnp.mean(x, axis=-1, keepdims=True)
  xc = x - mu
  var = jnp.mean(xc * xc, axis=-1, keepdims=True)
  return xc * lax.rsqrt(var + EPS_LN) * w + b


def _sgu_prompt_kernel(z_ref, lnw_ref, lnb_ref, ws_ref, bs_ref, oa_ref, *, d_a):
  dh = d_a // H_A
  ga = _gelu_exact(z_ref[...])
  u = ga[:, :d_a]
  van = _layer_norm(ga[:, d_a:], lnw_ref[...], lnb_ref[...]).astype(BF16)
  row = lax.broadcasted_iota(jnp.int32, (CHUNK, CHUNK), 0)
  col = lax.broadcasted_iota(jnp.int32, (CHUNK, CHUNK), 1)
  causal = row >= col
  for h in range(H_A):
    sl = slice(h * dh, (h + 1) * dh)
    w_h = jnp.where(causal, ws_ref[h], 0.0).astype(BF16)
    s_h = jnp.dot(w_h, van[:, sl], preferred_element_type=F32) + bs_ref[:, h:h + 1]
    oa_ref[:, sl] = (u[:, sl] * s_h).astype(oa_ref.dtype)


def _sgu_prompt(p, z, ln_w, ln_b, w_sp, b_sp, d_a):
  assert p.tm == CHUNK
  blocks = (_nbytes((CHUNK, 2 * d_a), F32) + _nbytes((H_A, CHUNK, CHUNK), F32)
            + _nbytes((CHUNK, d_a), BF16))
  return pl.pallas_call(
      functools.partial(_sgu_prompt_kernel, d_a=d_a),
      out_shape=jax.ShapeDtypeStruct((p.m, d_a), BF16),
      grid=(p.n_tiles,),
      in_specs=[p.row_spec(2 * d_a), _vec_spec(d_a), _vec_spec(d_a),
                _full_spec((H_A, CHUNK, CHUNK)), _full_spec((CHUNK, H_A))],
      out_specs=p.row_spec(d_a),
      compiler_params=_row_params(blocks, 4 * _nbytes((CHUNK, 2 * d_a), F32)),
      name=f"sgu_{p.name}",
  )(z, ln_w.reshape(1, d_a), ln_b.reshape(1, d_a), w_sp[:, :CHUNK, :CHUNK],
    b_sp[:, :CHUNK].T)


def _sgu_sample_kernel(z_ref, lnw_ref, lnb_ref, wsv_ref, bsv_ref, oa_ref, vn_ref, *, d_a, t, b):
  ga = _gelu_exact(z_ref[...])
  van = _layer_norm(ga[:, d_a:], lnw_ref[...], lnb_ref[...])
  vn_ref[...] = van
  for i in range(t):
    s = bsv_ref[i:i + 1, :]
    for j in range(i + 1):
      s = s + wsv_ref[i * t + j:i * t + j + 1, :] * van[j * b:(j + 1) * b, :]
    oa_ref[i * b:(i + 1) * b, :] = (ga[i * b:(i + 1) * b, :d_a] * s).astype(oa_ref.dtype)


def _sgu_sample(p, z, ln_w, ln_b, w_sp, b_sp, d_a):
  t, b = p.t, p.b
  dh = d_a // H_A
  wsv = jnp.repeat(w_sp[:, :t, :t].transpose(1, 2, 0).reshape(t * t, H_A), dh, axis=1)
  bsv = jnp.repeat(b_sp[:, :t].T, dh, axis=1)
  blocks = (_nbytes((p.m, 2 * d_a), F32) + _nbytes((p.m, d_a), BF16) + _nbytes((p.m, d_a), F32))
  return pl.pallas_call(
      functools.partial(_sgu_sample_kernel, d_a=d_a, t=t, b=b),
      out_shape=(jax.ShapeDtypeStruct((p.m, d_a), BF16), jax.ShapeDtypeStruct((p.m, d_a), F32)),
      grid=(1,),
      in_specs=[pl.BlockSpec((p.m, 2 * d_a), lambda i: (0, 0)), _vec_spec(d_a), _vec_spec(d_a),
                _full_spec((t * t, d_a)), _full_spec((t, d_a))],
      out_specs=(_full_spec((p.m, d_a)), _full_spec((p.m, d_a))),
      compiler_params=pltpu.CompilerParams(
          dimension_semantics=("arbitrary",),
          vmem_limit_bytes=_vmem_limit(blocks, 0, 3 * _nbytes((p.m, 2 * d_a), F32))),
      name=f"sgu_{p.name}",
  )(z, ln_w.reshape(1, d_a), ln_b.reshape(1, d_a), wsv, bsv)


def _rwkv_math(cur, delta, mu_ref, bias_ref, lora_refs, v_first):
  rc, kc, vc, xc = cur
  rd, kd, vd, xd = delta
  w1_ref, w2_ref, a1_ref, a2_ref, g1_ref, g2_ref, v1_ref, v2_ref = lora_refs

  def lora(x, m1_ref, m2_ref, mid=None):
    hcol = jnp.dot(x.astype(BF16), m1_ref[...], preferred_element_type=F32)
    if mid is not None:
      hcol = mid(hcol)
    return jnp.dot(hcol.astype(BF16), m2_ref[...], preferred_element_type=F32)

  mu = lambda i: mu_ref[i:i + 1, :]
  bias = lambda i: bias_ref[i:i + 1, :]
  lerp = lambda c, dl, m: c + dl * m

  r = lerp(rc, rd, mu(0))
  k = lerp(kc, kd, mu(1))
  v = lerp(vc, vd, mu(2))
  u = bias(0) + lora(lerp(xc, xd, mu(3)), w1_ref, w2_ref, jnp.tanh)
  w = jnp.exp(-jnp.exp(-jax.nn.softplus(-u) - 0.5))
  if v_first is not None:
    gate = jax.nn.sigmoid(bias(2) + lora(lerp(xc, xd, mu(6)), v1_ref, v2_ref))
    v = v + (v_first - v) * gate
  ag = jax.nn.sigmoid(bias(1) + lora(lerp(xc, xd, mu(4)), a1_ref, a2_ref))
  g = lora(lerp(xc, xd, mu(5)), g1_ref, g2_ref, jax.nn.sigmoid)
  return r, w, k, v, ag, g


def _rwkv_prep_sample_kernel(*refs, has_vfirst):
  cur_refs, prev_refs = refs[0:4], refs[4:8]
  mu_ref, bias_ref = refs[8:10]
  lora_refs = refs[10:18]
  pos = 18
  vf = None
  if has_vfirst:
    vf = refs[pos][...]
    pos += 1
  out_refs = refs[pos:pos + 6]
  cur = [c[...] for c in cur_refs]
  delta = [p[...] - c for p, c in zip(prev_refs, cur)]
  outs = _rwkv_math(cur, delta, mu_ref, bias_ref, lora_refs, vf)
  for o_ref, val in zip(out_refs, outs):
    o_ref[...] = val


def _rwkv_prep_sample(p, z, prev, mu7, bias3, loras, v_first, d_b, col0):
  has_vfirst = v_first is not None
  ins = [z] * 4 + [prev] * 4 + [mu7, bias3] + list(loras)
  specs = [p.row_spec(d_b, col0 + s) for s in range(4)] + [p.row_spec(d_b, s) for s in range(4)]
  specs += [_full_spec(mu7.shape), _full_spec(bias3.shape)] + [_full_spec(m.shape) for m in loras]
  if has_vfirst:
    ins.append(v_first)
    specs.append(p.row_spec(d_b))
  tile = _nbytes((p.tm, d_b), F32)
  out = jax.ShapeDtypeStruct((p.m, d_b), F32)
  return pl.pallas_call(
      functools.partial(_rwkv_prep_sample_kernel, has_vfirst=has_vfirst),
      out_shape=(out,) * 6,
      grid=(p.n_tiles,),
      in_specs=specs,
      out_specs=(p.row_spec(d_b),) * 6,
      compiler_params=_row_params(15 * tile + 8 * _nbytes(loras[0].shape, BF16), 10 * tile),
      name=f"rwkv_prep_{p.name}",
  )(*ins)


def _lane_parts(arrs, h_b):
  lane = lax.broadcasted_iota(jnp.int32, arrs[0].shape, 1)
  out = arrs[0]
  for k in range(1, len(arrs)):
    out = jnp.where(lane >= k * h_b, arrs[k], out)
  return out


def _interleave(vals, h_b):
  nb = len(vals)
  width = nb * h_b
  tiles = []
  for c in range(HEAD_B // nb):
    src = [v[:, c * width:(c + 1) * width] for v in vals]
    rolled = []
    for r in range(nb):
      merged = _lane_parts([src[(q + r) % nb] for q in range(nb)], h_b)
      rolled.append(merged if r == 0 else pltpu.roll(merged, r * h_b, axis=1))
    for q in range(nb):
      tiles.append(_lane_parts([rolled[(b - q) % nb] for b in range(nb)], h_b))
  return jnp.concatenate(tiles, axis=1)


def _deinterleave(row_ref, nb, h_b):
  width = nb * h_b
  cols = [[] for _ in range(nb)]
  for c in range(HEAD_B // nb):
    tiles = [row_ref[:, (nb * c + q) * width:(nb * c + q + 1) * width] for q in range(nb)]
    rolled = []
    for r in range(nb):
      merged = _lane_parts([tiles[(part + r) % nb] for part in range(nb)], h_b)
      rolled.append(merged if r == 0 else pltpu.roll(merged, r * h_b, axis=1))
    for b in range(nb):
      cols[b].append(_lane_parts([rolled[(q - b) % nb] for q in range(nb)], h_b))
  return [jnp.concatenate(cb, axis=1) for cb in cols]


def _rwkv_prep_prompt_kernel(*refs, nb, tm, h_b, has_vfirst, emit_v):
  cur_refs, shift0_ref = refs[0:4], refs[4]
  mu_ref, bias_ref = refs[5:7]
  lora_refs = refs[7:15]
  pos = 15
  vf = None
  rows = nb * tm
  if has_vfirst:
    vf = refs[pos][...].reshape(rows, -1)
    pos += 1
  row_out_refs = refs[pos:pos + 5]
  g_ref = refs[pos + 5]
  pos += 6
  if emit_v:
    v_tok_ref = refs[pos]
    pos += 1
  carry_ref = refs[pos]

  @pl.when(pl.program_id(0) == 0)
  def _():
    d_b = carry_ref.shape[2]
    for s in range(4):
      carry_ref[s] = shift0_ref[:, s * d_b:(s + 1) * d_b]

  cur, delta = [], []
  for s, c_ref in enumerate(cur_refs):
    c = c_ref[...].reshape(rows, -1)
    rolled = pltpu.roll(c, 1, axis=0)
    sub = V7X_SUBLANES
    first = lax.broadcasted_iota(jnp.int32, (sub, c.shape[1]), 0) == 0
    pieces = []
    for b in range(nb):
      head = jnp.where(first, carry_ref[s, b:b + 1, :], rolled[b * tm:b * tm + sub])
      pieces += [head, rolled[b * tm + sub:(b + 1) * tm]]
    prev = jnp.concatenate(pieces, axis=0)
    for b in range(nb):
      carry_ref[s, b:b + 1, :] = c_ref[b, tm - 1:tm, :]
    cur.append(c)
    delta.append(prev - c)
  r, w, k, v, ag, g = _rwkv_math(cur, delta, mu_ref, bias_ref, lora_refs, vf)
  for o_ref, val in zip(row_out_refs, (r, w, k, v, ag)):
    o_ref[...] = _interleave([val[b * tm:(b + 1) * tm] for b in range(nb)], h_b)
  g_ref[...] = g.reshape(nb, tm, -1)
  if emit_v:
    v_tok_ref[...] = v.reshape(nb, tm, -1)


def _rwkv_prep_prompt(p, zb, shift0, mu7, bias3, loras, v_first, d_b, emit_v):
  nb, t = p.b, p.t
  h_b = d_b // HEAD_B
  tm = _tile(t, PREP_TM, V7X_SUBLANES)
  has_vfirst = v_first is not None
  tok_spec = lambda c=0: pl.BlockSpec((nb, tm, d_b), lambda i, c=c: (0, i, c))
  ins = [zb] * 4 + [shift0, mu7, bias3] + list(loras)
  specs = [tok_spec(s) for s in range(4)]
  specs += [_full_spec(shift0.shape), _full_spec(mu7.shape), _full_spec(bias3.shape)]
  specs += [_full_spec(m.shape) for m in loras]
  if has_vfirst:
    ins.append(v_first)
    specs.append(tok_spec())
  width = HEAD_B * nb * h_b
  row_shape = jax.ShapeDtypeStruct((t, width), F32)
  tok_shape = jax.ShapeDtypeStruct((nb, t, d_b), F32)
  out_shape = (row_shape,) * 5 + (tok_shape,) + ((tok_shape,) if emit_v else ())
  out_specs = ((pl.BlockSpec((tm, width), lambda i: (i, 0)),) * 5 + (tok_spec(),)
               + ((tok_spec(),) if emit_v else ()))
  tile = _nbytes((nb * tm, d_b), F32)
  return pl.pallas_call(
      functools.partial(_rwkv_prep_prompt_kernel, nb=nb, tm=tm, h_b=h_b, has_vfirst=has_vfirst,
                        emit_v=emit_v),
      out_shape=out_shape,
      grid=(t // tm,),
      in_specs=specs,
      out_specs=out_specs,
      scratch_shapes=[pltpu.VMEM((4, nb, d_b), F32)],
      compiler_params=pltpu.CompilerParams(
          dimension_semantics=("arbitrary",),
          vmem_limit_bytes=_vmem_limit(12 * tile + 8 * _nbytes(loras[0].shape, BF16), 0,
                                       16 * tile)),
      name=f"rwkv_prep_{p.name}",
  )(*ins)


def _wkv_kernel(r_ref, w_ref, k_ref, v_ref, ag_ref, kk_ref, ka_ref, rk_ref, lnw_ref, lnb_ref,
                s0_ref, o_ref, s_ref, a_scr, b_scr, km_scr, yr_scr, vt_scr, yt_scr, *, tb, n, l):
  @pl.when(pl.program_id(1) == 0)
  def _():
    s_ref[...] = s0_ref[...]

  def tile_sum(x):
    acc = x[:, 0:l]
    for j in range(1, n):
      acc = acc + x[:, j * l:(j + 1) * l]
    return acc

  rep = lambda x: jnp.concatenate([x] * n, axis=1)
  row = lambda ref, u, j: ref[u:u + 1, j * l:(j + 1) * l]

  k_raw = k_ref[...]
  ag = ag_ref[...]
  kk = k_raw * kk_ref[...]
  kk = kk / rep(jnp.maximum(jnp.sqrt(tile_sum(kk * kk)), 1e-12))
  a_scr[0:tb, :] = -kk
  a_scr[tb:tb + V7X_SUBLANES, :] = jnp.zeros((V7X_SUBLANES, n * l), F32)
  b_scr[...] = kk * ag
  km_scr[...] = k_raw * (1.0 + (ag - 1.0) * ka_ref[...])

  sa0 = jnp.zeros((n, l), F32)
  for j in range(n):
    sa0 = sa0 + s_ref[j] * row(a_scr, 0, j)

  group = V7X_SUBLANES if tb % V7X_SUBLANES == 0 else tb

  def step_group(gi, sa):
    if tb == group:
      win = lambda ref, off=0: ref
      a_next = None
    else:
      base = pl.multiple_of(gi * group, group)
      win = lambda ref, off=0: ref.at[pl.ds(pl.multiple_of(base + off, group), group), :]
      a_next = win(a_scr, group)
    w_w, b_w, k_w, r_w, v_w, a_w, y_w = (win(w_ref), win(b_scr), win(km_scr), win(r_ref),
                                         win(v_ref), win(a_scr), win(yr_scr))
    for u in range(group):
      for i in range(n):
        vt_scr[i:i + 1, :] = row(v_w, u, i)
      v_t = vt_scr[...]
      y = jnp.zeros_like(sa)
      sa_next = jnp.zeros_like(sa)
      for j in range(n):
        s_new = s_ref[j] * row(w_w, u, j) + sa * row(b_w, u, j) + v_t * row(k_w, u, j)
        s_ref[j] = s_new
        y = y + s_new * row(r_w, u, j)
        if u + 1 < group or a_next is None:
          sa_next = sa_next + s_new * row(a_w, u + 1, j)
        else:
          sa_next = sa_next + s_new * row(a_next, 0, j)
      yt_scr[...] = y
      for i in range(n):
        y_w[u:u + 1, i * l:(i + 1) * l] = yt_scr[i:i + 1, :]
      sa = sa_next
    return sa

  if tb == group:
    step_group(0, sa0)
  else:
    lax.fori_loop(0, tb // group, step_group, sa0)

  y = yr_scr[...]
  yc = y - rep(tile_sum(y) * (1.0 / n))
  var = tile_sum(yc * yc) * (1.0 / n)
  yn = yc * rep(lax.rsqrt(var + EPS_GN)) * lnw_ref[...] + lnb_ref[...]
  rk = tile_sum(r_ref[...] * km_scr[...] * rk_ref[...])
  o_ref[...] = yn + rep(rk) * v_ref[...]


def _wkv(seq, params, s0, *, tb, name):
  g, t, width = seq[0].shape
  n = HEAD_B
  l = width // n
  tb = _tile(t, tb, V7X_SUBLANES)
  seq_spec = pl.BlockSpec((None, tb, width), lambda gi, ti: (gi, ti, 0))
  par_spec = pl.BlockSpec((None, 1, width), lambda gi, ti: (gi, 0, 0))
  st_spec = pl.BlockSpec((None, n, n, l), lambda gi, ti: (gi, 0, 0, 0))
  blk = _nbytes((tb, width), F32)
  st = _nbytes((n, n, l), F32)
  return pl.pallas_call(
      functools.partial(_wkv_kernel, tb=tb, n=n, l=l),
      out_shape=(jax.ShapeDtypeStruct((g, t, width), F32), jax.ShapeDtypeStruct((g, n, n, l), F32)),
      grid=(g, t // tb),
      in_specs=[seq_spec] * 5 + [par_spec] * 5 + [st_spec],
      out_specs=(seq_spec, st_spec),
      scratch_shapes=[pltpu.VMEM((tb + V7X_SUBLANES, width), F32), pltpu.VMEM((tb, width), F32),
                      pltpu.VMEM((tb, width), F32), pltpu.VMEM((tb, width), F32),
                      pltpu.VMEM((n, l), F32), pltpu.VMEM((n, l), F32)],
      compiler_params=pltpu.CompilerParams(
          dimension_semantics=("parallel", "arbitrary"),
          vmem_limit_bytes=_vmem_limit(6 * blk + 2 * st + 5 * _nbytes((1, width), F32),
                                       6 * blk, 10 * blk)),
      name=name,
  )(*seq, *params, s0)


def _cat_sample_kernel(oa_ref, ob_ref, g_ref, o_ref, *, d_a):
  o_ref[:, :d_a] = oa_ref[...]
  o_ref[:, d_a:] = (ob_ref[...] * g_ref[...]).astype(o_ref.dtype)


def _cat_sample(p, oa, ob, g, d_a, d_b):
  d = d_a + d_b
  blocks = _nbytes((p.tm, d_a), BF16) + 2 * _nbytes((p.tm, d_b), F32) + _nbytes((p.tm, d), BF16)
  return pl.pallas_call(
      functools.partial(_cat_sample_kernel, d_a=d_a),
      out_shape=jax.ShapeDtypeStruct((p.m, d), BF16),
      grid=(p.n_tiles,),
      in_specs=[p.row_spec(d_a), p.row_spec(d_b), p.row_spec(d_b)],
      out_specs=p.row_spec(d),
      compiler_params=_row_params(blocks, _nbytes((p.tm, d_b), F32)),
      name=f"cat_{p.name}",
  )(oa, ob, g)


def _cat_prompt_kernel(oa_ref, ob_ref, g_ref, o_ref, *, nb, h_b, d_a):
  o_ref[:, :, :d_a] = oa_ref[...]
  for b, ob in enumerate(_deinterleave(ob_ref, nb, h_b)):
    o_ref[b, :, d_a:] = (ob * g_ref[b]).astype(o_ref.dtype)


def _cat_prompt(p, oa, ob, g, d_a, d_b):
  nb, t = p.b, p.t
  h_b = d_b // HEAD_B
  d = d_a + d_b
  tm = _tile(t, CAT_TM, V7X_SUBLANES)
  width = ob.shape[1]
  blocks = (_nbytes((nb, tm, d_a), BF16) + _nbytes((tm, width), F32) + _nbytes((nb, tm, d_b), F32)
            + _nbytes((nb, tm, d), BF16))
  return pl.pallas_call(
      functools.partial(_cat_prompt_kernel, nb=nb, h_b=h_b, d_a=d_a),
      out_shape=jax.ShapeDtypeStruct((nb, t, d), BF16),
      grid=(t // tm,),
      in_specs=[pl.BlockSpec((nb, tm, d_a), lambda i: (0, i, 0)),
                pl.BlockSpec((tm, width), lambda i: (i, 0)),
                pl.BlockSpec((nb, tm, d_b), lambda i: (0, i, 0))],
      out_specs=pl.BlockSpec((nb, tm, d), lambda i: (0, i, 0)),
      compiler_params=_row_params(blocks, 3 * _nbytes((tm, width), F32)),
      name=f"cat_{p.name}",
  )(oa, ob, g)


def _to_jh(x, h_b):
  s = x.shape[:-1]
  return x.reshape(s + (h_b, HEAD_B)).swapaxes(-1, -2).reshape(s + (h_b * HEAD_B,))


def _from_jh(x, h_b):
  s = x.shape[:-1]
  return x.reshape(s + (HEAD_B, h_b)).swapaxes(-1, -2).reshape(s + (h_b * HEAD_B,))


def _slabs_to_jh(x, h_b):
  s = x.shape[:-1]
  return _to_jh(x.reshape(s + (4, h_b * HEAD_B)), h_b).reshape(s + (4 * h_b * HEAD_B,))


def _slabs_from_jh(x, h_b):
  s = x.shape[:-1]
  return _from_jh(x.reshape(s + (4, h_b * HEAD_B)), h_b).reshape(s + (4 * h_b * HEAD_B,))


def _scan_param(p, v, h_b):
  vh = v.reshape(h_b, HEAD_B)
  if p.time_major:
    return jnp.repeat(vh, p.b, axis=1).reshape(h_b, 1, HEAD_B * p.b)
  return jnp.tile(vh.T, (1, p.b)).reshape(1, 1, HEAD_B * p.b * h_b)


def _state_to_scan(p, s):
  if p.time_major:
    return s.transpose(1, 3, 2, 0)
  b, h_b = s.shape[:2]
  return s.transpose(3, 2, 0, 1).reshape(1, HEAD_B, HEAD_B, b * h_b)


def _state_from_scan(p, s, h_b):
  if p.time_major:
    return s.transpose(3, 0, 2, 1)
  return s.reshape(HEAD_B, HEAD_B, p.b, h_b).transpose(2, 3, 1, 0)


def _trunk(p, x, mod_l, shift0, wkv0, weights, keep_chunk_rows):
  (g_pre_mix, g_post_mix, g_pre_ffn, g_post_ffn, w_in_a, w_in_b, w_out_a, w_out_b, ln_v_w, ln_v_b,
   w_spatial, b_spatial, mu7s, bias3s, loras, k_k, k_a, r_k, ln_x_w, ln_x_b, w_up, w_down) = weights
  depth = w_in_a.shape[0]
  d = p.d
  d_a = d // 2
  d_b = d - d_a
  h_b = d_b // HEAD_B
  mm_tm = MM_TM_SAMPLE if p.time_major else MM_TM
  new_wkv, new_shift, chunk_rows = [], [], []
  v_first = None
  h = _prenorm(p, x, mod_l[0], g_pre_mix[0], 1, 0)
  for l in range(depth):
    mod = mod_l[l]
    za = _matmul(h, w_in_a, l, F32, tm=mm_tm, name=f"mm_in_a_{p.name}")
    zb = _matmul(h, w_in_b, l, F32, tm=mm_tm, name=f"mm_in_b_{p.name}")
    par = [_scan_param(p, t, h_b) for t in (k_k[l], k_a[l], r_k[l].reshape(-1), ln_x_w[l], ln_x_b[l])]
    s0 = _state_to_scan(p, wkv0[l])
    if p.time_major:
      out_a, va_n = _sgu_sample(p, za, ln_v_w[l], ln_v_b[l], w_spatial[l], b_spatial[l], d_a)
      if keep_chunk_rows:
        chunk_rows.append(va_n)
      prev = jnp.concatenate([shift0[l], zb[:-p.b]], axis=0)
      new_shift.append(zb[(p.t - 1) * p.b:])
      r, w, k, v, ag, g = _rwkv_prep_sample(p, zb, prev, mu7s[l], bias3s[l], loras[l], v_first,
                                            d_b, 0)
      if l == 0:
        v_first = v
      to_scan = lambda a: (a.reshape(p.t, p.b, HEAD_B, h_b).transpose(3, 0, 2, 1)
                           .reshape(h_b, p.t, HEAD_B * p.b))
      ob, s_new = _wkv([to_scan(a) for a in (r, w, k, v, ag)], par, s0, tb=WKV_TB,
                       name=f"wkv_{p.name}")
      ob = ob.reshape(h_b, p.t, HEAD_B, p.b).transpose(1, 3, 2, 0).reshape(p.m, d_b)
      cat = _cat_sample(p, out_a, ob, g, d_a, d_b)
    else:
      out_a = _sgu_prompt(p, za, ln_v_w[l], ln_v_b[l], w_spatial[l], b_spatial[l], d_a)
      zb3 = zb.reshape(p.b, p.t, 4 * d_b)
      new_shift.append(zb3[:, p.t - 1])
      outs = _rwkv_prep_prompt(p, zb3, shift0[l], mu7s[l], bias3s[l], loras[l], v_first, d_b,
                               emit_v=(l == 0))
      if l == 0:
        v_first = outs[6]
      seq = [a[None] for a in outs[:5]]
      ob, s_new = _wkv(seq, par, s0, tb=WKV_TB, name=f"wkv_{p.name}")
      cat = _cat_prompt(p, out_a.reshape(p.b, p.t, d_a), ob[0], outs[5], d_a, d_b)
      cat = cat.reshape(p.m, d)
    new_wkv.append(_state_from_scan(p, s_new, h_b))
    y = _matmul_pair(cat, d_a, w_out_a, w_out_b, l, F32, tm=mm_tm, name=f"mm_out_{p.name}")
    x, h = _resid_prenorm(p, x, y, mod, 2, g_post_mix[l], mod, g_pre_ffn[l], 4, 3)
    f = _matmul(h, w_up, l, BF16, act=_relu2, tm=mm_tm, name=f"mm_up_{p.name}")
    f = _matmul(f, w_down, l, F32, tm=mm_tm, name=f"mm_down_{p.name}")
    if l + 1 < depth:
      x, h = _resid_prenorm(p, x, f, mod, 5, g_post_ffn[l], mod_l[l + 1], g_pre_mix[l + 1], 1, 0)
    else:
      x = _resid(p, x, f, mod, 5, g_post_ffn[l])
  rows = jnp.stack(chunk_rows) if keep_chunk_rows else None
  return x, jnp.stack(new_wkv), jnp.stack(new_shift), rows


def _pad_lora(m1, m2, h_b):
  rank = m1.shape[1]
  rp = -(-rank // V7X_LANES) * V7X_LANES
  m1 = jnp.pad(_to_jh(m1.T, h_b).T, ((0, 0), (0, rp - rank))).astype(BF16)
  m2 = jnp.pad(_to_jh(m2, h_b), ((0, rp - rank), (0, 0))).astype(BF16)
  return m1, m2


def kernel(x_prompt, x_sample, state_wkv, state_shift, c_prompt, c_sample, w_ada, b_ada, g_pre_mix, g_post_mix, g_pre_ffn, g_post_ffn, w_in, w_out, ln_v_w, ln_v_b, w_spatial, b_spatial, mu_rkv, mu_lora, mu_vm, w0, w1, w2, a0, a1, a2, v0, v1, v2, g1, g2, k_k, k_a, r_k, ln_x_w, ln_x_b, w_up, w_down):
  bp, tp, d = x_prompt.shape
  bs, ts, _ = x_sample.shape
  depth = w_in.shape[0]
  d_a = d // 2
  d_b = d - d_a
  h_b = d_b // HEAD_B
  p_a = 2 * d_a

  n_c = bs + bp
  c_all = jnp.concatenate([c_sample, c_prompt, jnp.zeros((-n_c % V7X_SUBLANES, d), F32)], axis=0)
  mod_all = _ada_all(c_all, w_ada, b_ada)
  mod_s = [mod_all[l, :bs] for l in range(depth)]
  mod_p = [mod_all[l, bs:n_c].reshape(bp, 1, N_MOD * d) for l in range(depth)]

  w_in_a = _cast_cols(w_in, 0, p_a, "cast_w_in_a")
  w_in_b = _slabs_to_jh(_cast_cols(w_in, p_a, 4 * d_b, "cast_w_in_b"), h_b)
  w_out_a = w_out[:, :d_a].astype(BF16)
  w_out_b = (w_out[:, d_a:].reshape(depth, h_b, HEAD_B, d).swapaxes(1, 2).reshape(depth, d_b, d)
             .astype(BF16))
  loras, mu7s, bias3s = [], [], []
  for l in range(depth):
    lv = max(l - 1, 0)
    loras.append(_pad_lora(w1[l], w2[l], h_b) + _pad_lora(a1[l], a2[l], h_b)
                 + _pad_lora(g1[l], g2[l], h_b) + _pad_lora(v1[lv], v2[lv], h_b))
    mu7s.append(_to_jh(jnp.concatenate([mu_rkv[l], mu_lora[l], mu_vm[lv][None]], axis=0), h_b))
    bias3s.append(_to_jh(jnp.stack([w0[l], a0[l], v0[lv]]), h_b))
  weights = (g_pre_mix, g_post_mix, g_pre_ffn, g_post_ffn, w_in_a, w_in_b, w_out_a, w_out_b,
             ln_v_w, ln_v_b, w_spatial, b_spatial, mu7s, bias3s, loras, k_k, k_a, r_k, ln_x_w,
             ln_x_b, w_up.astype(BF16), w_down.astype(BF16))

  pp = _Pass("prompt", bp, tp, d, time_major=False)
  shift0 = jnp.zeros((depth, bp, 4 * d_b), x_prompt.dtype)
  wkv0 = jnp.zeros((depth, bp, h_b, HEAD_B, HEAD_B), state_wkv.dtype)
  y_p, wkv_p, shift_p, _ = _trunk(pp, x_prompt.reshape(bp * tp, d), mod_p, shift0, wkv0,
                                  weights, False)

  ps = _Pass("sample", bs, ts, d, time_major=True)
  x_s = x_sample.transpose(1, 0, 2).reshape(ts * bs, d)
  y_s, wkv_s, shift_s, rows_s = _trunk(ps, x_s, mod_s, _slabs_to_jh(state_shift, h_b), state_wkv,
                                       weights, True)

  y_sample = y_s.reshape(ts, bs, d).transpose(1, 0, 2)
  chunk_v = rows_s.reshape(depth, ts, bs, d_a).transpose(0, 2, 1, 3)
  return (y_p.reshape(bp, tp, d), y_sample, wkv_p, _slabs_from_jh(shift_p, h_b), wkv_s,
          _slabs_from_jh(shift_s, h_b), chunk_v)
```
